```python
import math
import jax
import jax.numpy as jnp
from jax import lax
import numpy as np

D_MODEL = 1024
BATCH = 2
SEQ = 8192
DEPTH = 4

D_A = D_MODEL // 2
GMLP_GROUPS = 4
GMLP_CHUNK = 128
D_B = D_MODEL // 2
CONV_WIDTH = 31
HEAD_DIM = 64
N_HEADS = D_MODEL // 128
D_C = N_HEADS * HEAD_DIM
MOBA_BLOCK = 256
MOBA_TOPK = 3
Q_CHUNK = 128
N_BRANCH = 3
D_BRANCH = 512
W_IN_COLS = 3 * D_A + 3 * D_B + 4 * D_C + N_BRANCH * D_MODEL
ALPHA = (2 * DEPTH) ** 0.25
BETA = (8 * DEPTH) ** -0.25
LN_EPS = 1e-5

kernel_name = "hybrid_gmlp_conformer_moba_deepnorm"


def layer_norm(x, g, b):
    xf = x.astype(jnp.float32)
    mu = jnp.mean(xf, axis=-1, keepdims=True)
    var = jnp.mean(jnp.square(xf - mu), axis=-1, keepdims=True)
    y = (xf - mu) * lax.rsqrt(var + LN_EPS)
    return (y * g + b).astype(x.dtype)


def gmlp_spatial(v, sg_w, sg_b):
    B, S, _ = v.shape
    nc = S // GMLP_CHUNK
    cg = D_A // GMLP_GROUPS
    mask = jnp.tril(jnp.ones((GMLP_CHUNK, GMLP_CHUNK), dtype=bool))
    wm = jnp.where(mask[None], sg_w, jnp.zeros_like(sg_w))
    vc = v.reshape(B, nc, GMLP_CHUNK, GMLP_GROUPS, cg)
    z = jnp.einsum('gts,bnsgc->bntgc', wm, vc) + sg_b.T[None, None, :, :, None]
    return z.reshape(B, S, D_A)


def causal_depthwise_conv(x, w, b):
    C = x.shape[-1]
    y = lax.conv_general_dilated(
        x, w[:, None, :], window_strides=(1,), padding=[(CONV_WIDTH - 1, 0)],
        dimension_numbers=('NWC', 'WIO', 'NWC'), feature_group_count=C)
    return y + b


def moba_attention(q, k, v):
    B, S, H, Dh = q.shape
    nb = -(-S // MOBA_BLOCK)
    pad = nb * MOBA_BLOCK - S
    topk = min(MOBA_TOPK, nb)
    q = q.transpose(0, 2, 1, 3)
    k = jnp.pad(k.transpose(0, 2, 1, 3), ((0, 0), (0, 0), (0, pad), (0, 0)))
    v = jnp.pad(v.transpose(0, 2, 1, 3), ((0, 0), (0, 0), (0, pad), (0, 0)))
    kb = k.reshape(B, H, nb, MOBA_BLOCK, Dh)
    vb = v.reshape(B, H, nb, MOBA_BLOCK, Dh)
    kmean = jnp.mean(kb.astype(jnp.float32), axis=3)
    gate = jnp.einsum('bhsd,bhnd->bhsn', q.astype(jnp.float32), kmean)
    q_blk = jnp.arange(S) // MOBA_BLOCK
    past = jnp.arange(nb)[None, :] < q_blk[:, None]
    gate = jnp.where(past[None, None], gate, -jnp.inf)
    _, sel = lax.top_k(gate, topk)

    nq = S // Q_CHUNK
    q_c = q.reshape(B, H, nq, Q_CHUNK, Dh).transpose(2, 0, 1, 3, 4)
    sel_c = sel.reshape(B, H, nq, Q_CHUNK, topk).transpose(2, 0, 1, 3, 4)
    slopes = 2.0 ** (-(jnp.arange(1, H + 1, dtype=jnp.float32) * 8.0 / H))
    scale = Dh ** -0.5
    bi = jnp.arange(B)[:, None, None, None]
    hi = jnp.arange(H)[None, :, None, None]
    kpos = jnp.arange(MOBA_BLOCK)

    def one_chunk(args):
        qc, selc, c = args
        t = c * Q_CHUNK + jnp.arange(Q_CHUNK)
        blk = (c * Q_CHUNK) // MOBA_BLOCK
        kg = kb[bi, hi, selc]
        vg = vb[bi, hi, selc]
        lp = jnp.einsum('bhqd,bhqnkd->bhqnk', qc, kg).astype(jnp.float32) * scale
        spos = selc[..., None] * MOBA_BLOCK + kpos
        dist_p = (t[None, None, :, None, None] - spos).astype(jnp.float32)
        lp = lp - slopes[None, :, None, None, None] * dist_p
        valid = (jnp.arange(topk) < blk)[None, None, None, :, None]
        lp = jnp.where(valid, lp, -jnp.inf).reshape(B, H, Q_CHUNK, topk * MOBA_BLOCK)
        ko = lax.dynamic_index_in_dim(kb, blk, axis=2, keepdims=False)
        vo = lax.dynamic_index_in_dim(vb, blk, axis=2, keepdims=False)
        lo = jnp.einsum('bhqd,bhkd->bhqk', qc, ko).astype(jnp.float32) * scale
        dist_o = t[:, None] - (blk * MOBA_BLOCK + kpos)[None, :]
        lo = jnp.where(dist_o[None, None] >= 0,
                       lo - slopes[None, :, None, None] * dist_o.astype(jnp.float32)[None, None],
                       -jnp.inf)
        p = jax.nn.softmax(jnp.concatenate([lp, lo], axis=-1), axis=-1).astype(vb.dtype)
        out = jnp.einsum('bhqn,bhqnd->bhqd', p[..., :topk * MOBA_BLOCK],
                         vg.reshape(B, H, Q_CHUNK, topk * MOBA_BLOCK, Dh))
        out = out + jnp.einsum('bhqk,bhkd->bhqd', p[..., topk * MOBA_BLOCK:], vo)
        return out

    out = lax.map(one_chunk, (q_c, sel_c, jnp.arange(nq)))
    return out.transpose(1, 0, 3, 2, 4).reshape(B, S, H * Dh)


def hybrid_layer(x, w_in, sg_w, sg_b, v_ln_g, v_ln_b, conv_w, conv_b, cv_ln_g, cv_ln_b,
                 w_branch, w_out, ln_g, ln_b):
    B, S, D = x.shape
    h = x @ w_in
    cuts = np.cumsum([D_A, D_A, D_A, D_B, D_B, D_B, D_C, D_C, D_C, D_C])
    (a_u, a_v, a_g, b_val, b_glu, b_g, c_q, c_k, c_v, c_g, merge) = jnp.split(h, cuts, axis=-1)
    y_a = a_u * gmlp_spatial(layer_norm(a_v, v_ln_g, v_ln_b), sg_w, sg_b)
    y_a = y_a * jax.nn.silu(a_g)
    glu = b_val * jax.nn.sigmoid(b_glu)
    cv = jax.nn.silu(layer_norm(causal_depthwise_conv(glu, conv_w, conv_b), cv_ln_g, cv_ln_b))
    y_b = cv * jax.nn.silu(b_g)
    att = moba_attention(c_q.reshape(B, S, N_HEADS, HEAD_DIM),
                         c_k.reshape(B, S, N_HEADS, HEAD_DIM),
                         c_v.reshape(B, S, N_HEADS, HEAD_DIM))
    y_c = att * jax.nn.silu(c_g)
    ys = jnp.stack([y_a, y_b, y_c], axis=2)
    branch = jnp.einsum('bsnc,ncd->bsnd', ys, w_branch)
    gates = jax.nn.sigmoid(merge.reshape(B, S, N_BRANCH, D))
    mixed = jnp.sum(gates * branch, axis=2)
    out = mixed @ w_out
    return layer_norm(ALPHA * x + out, ln_g, ln_b)


def setup_inputs(seed: int = 0) -> dict:
    key = jax.random.key(seed)
    ks = jax.random.split(key, 16)
    f32 = jnp.float32
    cg_t = GMLP_CHUNK
    return {
        "x": jax.random.normal(ks[0], (BATCH, SEQ, D_MODEL), f32),
        "w_in": jax.random.normal(ks[1], (DEPTH, D_MODEL, W_IN_COLS), f32) * D_MODEL ** -0.5,
        "sg_w": jax.random.normal(ks[2], (DEPTH, GMLP_GROUPS, cg_t, cg_t), f32) * cg_t ** -0.5,
        "sg_b": 1.0 + 0.01 * jax.random.normal(ks[3], (DEPTH, GMLP_GROUPS, cg_t), f32),
        "v_ln_g": 1.0 + 0.02 * jax.random.normal(ks[4], (DEPTH, D_A), f32),
        "v_ln_b": 0.02 * jax.random.normal(ks[5], (DEPTH, D_A), f32),
        "conv_w": jax.random.normal(ks[6], (DEPTH, CONV_WIDTH, D_B), f32) * CONV_WIDTH ** -0.5,
        "conv_b": 0.02 * jax.random.normal(ks[7], (DEPTH, D_B), f32),
        "cv_ln_g": 1.0 + 0.02 * jax.random.normal(ks[8], (DEPTH, D_B), f32),
        "cv_ln_b": 0.02 * jax.random.normal(ks[9], (DEPTH, D_B), f32),
        "w_branch": jax.random.normal(ks[10], (DEPTH, N_BRANCH, D_BRANCH, D_MODEL), f32) * (D_BRANCH ** -0.5) * BETA,
        "w_out": jax.random.normal(ks[11], (DEPTH, D_MODEL, D_MODEL), f32) * (D_MODEL ** -0.5) * BETA,
        "ln_g": 1.0 + 0.02 * jax.random.normal(ks[12], (DEPTH, D_MODEL), f32),
        "ln_b": 0.02 * jax.random.normal(ks[13], (DEPTH, D_MODEL), f32),
    }


def reference(x, w_in, sg_w, sg_b, v_ln_g, v_ln_b, conv_w, conv_b, cv_ln_g, cv_ln_b,
              w_branch, w_out, ln_g, ln_b):
    for l in range(DEPTH):
        x = hybrid_layer(x, w_in[l], sg_w[l], sg_b[l], v_ln_g[l], v_ln_b[l], conv_w[l], conv_b[l],
                         cv_ln_g[l], cv_ln_b[l], w_branch[l], w_out[l], ln_g[l], ln_b[l])
    return x
```

```python
import functools
import math

import jax
import jax.numpy as jnp
from jax import lax
from jax.experimental import pallas as pl
from jax.experimental.pallas import tpu as pltpu

F32 = jnp.float32
BF16 = jnp.bfloat16

D_MODEL = 1024
D_BRANCH = 512
GMLP_GROUPS = 4
GMLP_CHUNK = 128
CONV_WIDTH = 31
CONV_HALO = 32
HEAD_DIM = 64
N_HEADS = 8
MOBA_BLOCK = 256
MOBA_TOPK = 3
N_BRANCH = 3
W_IN_COLS = 10 * D_BRANCH + N_BRANCH * D_MODEL
LN_EPS = 1e-5
MASK_VALUE = -1e30
LANES = 128
VMEM_LIMIT_BYTES = 56 * 1024 * 1024

_C_AU, _C_AV, _C_AG = 0, 512, 1024
_C_BVAL, _C_BGLU, _C_BG = 1536, 2048, 2560
_C_Q, _C_K, _C_V, _C_CG = 3072, 3584, 4096, 4608
_C_MERGE = 5120

_PC_ONE_A, _PC_ONE_B, _PC_BLK, _PC_POS = 32, 33, 34, 35


def _sigmoid(x):
    return 1.0 / (1.0 + jnp.exp(-x))


def _silu(x):
    return x * _sigmoid(x)


def _layer_norm(x, g, b):
    mu = jnp.mean(x, axis=-1, keepdims=True)
    xc = x - mu
    var = jnp.mean(xc * xc, axis=-1, keepdims=True)
    return xc * lax.rsqrt(var + LN_EPS) * g + b


def _dot(a, b):
    return jnp.dot(a, b, preferred_element_type=F32)


def _dot_nt(a, b):
    return lax.dot_general(a, b, (((1,), (1,)), ((), ())), preferred_element_type=F32)


def _proj_mix_kernel(x_ref, w_in_ref, sg_w_ref, sg_b_ref, vg_ref, vb_ref, cw_ref, cb_ref,
                     cg_ref, cbeta_ref, wbr_ref,
                     mixed_ref, gate_c_ref, q_ref, k_ref, v_ref, gc_ref,
                     conv_buf, conv_out, *, tm):
    i = pl.program_id(1)
    xb = x_ref[...].astype(BF16)

    def proj(c0, width):
        return _dot(xb, w_in_ref[:, c0:c0 + width])

    vn = _layer_norm(proj(_C_AV, D_BRANCH), vg_ref[...], vb_ref[...]).astype(BF16)
    row = lax.broadcasted_iota(jnp.int32, (GMLP_CHUNK, GMLP_CHUNK), 0)
    col = lax.broadcasted_iota(jnp.int32, (GMLP_CHUNK, GMLP_CHUNK), 1)
    cgw = D_BRANCH // GMLP_GROUPS
    z_rows = []
    for c in range(tm // GMLP_CHUNK):
        z_groups = []
        for g in range(GMLP_GROUPS):
            wm = jnp.where(col <= row, sg_w_ref[g], 0.0).astype(BF16)
            vcg = vn[c * GMLP_CHUNK:(c + 1) * GMLP_CHUNK, g * cgw:(g + 1) * cgw]
            z_groups.append(_dot(wm, vcg) + sg_b_ref[g])
        z_rows.append(jnp.concatenate(z_groups, axis=1))
    z = jnp.concatenate(z_rows, axis=0)
    y_a = proj(_C_AU, D_BRANCH) * z * _silu(proj(_C_AG, D_BRANCH))
    mixed = _sigmoid(proj(_C_MERGE, D_MODEL)) * _dot(y_a.astype(BF16), wbr_ref[0])

    @pl.when(i == 0)
    def _():
        conv_buf[0:CONV_HALO, :] = jnp.zeros((CONV_HALO, D_BRANCH), F32)

    conv_buf[CONV_HALO:CONV_HALO + tm, :] = proj(_C_BVAL, D_BRANCH) * _sigmoid(proj(_C_BGLU, D_BRANCH))
    rc = 64
    first = CONV_HALO - (CONV_WIDTH - 1)
    for r in range(tm // rc):
        for lc in range(D_BRANCH // LANES):
            ls = slice(lc * LANES, (lc + 1) * LANES)
            acc = jnp.zeros((rc, LANES), F32)
            for j in range(CONV_WIDTH):
                r0 = first + r * rc + j
                acc = acc + conv_buf[r0:r0 + rc, ls] * cw_ref[j:j + 1, ls]
            conv_out[r * rc:(r + 1) * rc, ls] = acc
    conv = conv_out[...] + cb_ref[...]
    conv_buf[0:CONV_HALO, :] = conv_buf[tm:tm + CONV_HALO, :]
    cv = _silu(_layer_norm(conv, cg_ref[...], cbeta_ref[...]))
    y_b = cv * _silu(proj(_C_BG, D_BRANCH))
    mixed = mixed + _sigmoid(proj(_C_MERGE + D_MODEL, D_MODEL)) * _dot(y_b.astype(BF16), wbr_ref[1])
    mixed_ref[...] = mixed

    gate_c_ref[...] = _sigmoid(proj(_C_MERGE + 2 * D_MODEL, D_MODEL)).astype(BF16)
    q_ref[...] = (proj(_C_Q, D_BRANCH) * (HEAD_DIM ** -0.5)).astype(BF16)
    k_ref[...] = proj(_C_K, D_BRANCH).astype(BF16)
    v_ref[...] = proj(_C_V, D_BRANCH).astype(BF16)
    gc_ref[...] = _silu(proj(_C_CG, D_BRANCH)).astype(BF16)


def _proj_mix(x, w_in, sg_w, sg_b_bc, v_ln_g, v_ln_b, conv_w, conv_b, cv_ln_g, cv_ln_b, w_branch, *, layer, tm):
    B, S, D = x.shape
    grid = (B, S // tm)
    row_spec = lambda width: pl.BlockSpec((None, tm, width), lambda b, i: (b, i, 0))
    full = lambda shape: pl.BlockSpec(shape, lambda b, i: (0,) * len(shape))
    vec = full((1, D_BRANCH))
    return pl.pallas_call(
        functools.partial(_proj_mix_kernel, tm=tm),
        grid=grid,
        in_specs=[
            row_spec(D),
            pl.BlockSpec((None, D, W_IN_COLS), lambda b, i: (layer, 0, 0), pipeline_mode=pl.Buffered(1)),
            full((GMLP_GROUPS, GMLP_CHUNK, GMLP_CHUNK)),
            full((GMLP_GROUPS, GMLP_CHUNK, GMLP_CHUNK)),
            vec, vec,
            full((CONV_WIDTH, D_BRANCH)),
            vec, vec, vec,
            pl.BlockSpec((None, 2, D_BRANCH, D_MODEL), lambda b, i: (layer, 0, 0, 0)),
        ],
        out_specs=[row_spec(D_MODEL), row_spec(D_MODEL), row_spec(D_BRANCH), row_spec(D_BRANCH),
                   row_spec(D_BRANCH), row_spec(D_BRANCH)],
        out_shape=[
            jax.ShapeDtypeStruct((B, S, D_MODEL), F32),
            jax.ShapeDtypeStruct((B, S, D_MODEL), BF16),
            jax.ShapeDtypeStruct((B, S, D_BRANCH), BF16),
            jax.ShapeDtypeStruct((B, S, D_BRANCH), BF16),
            jax.ShapeDtypeStruct((B, S, D_BRANCH), BF16),
            jax.ShapeDtypeStruct((B, S, D_BRANCH), BF16),
        ],
        scratch_shapes=[pltpu.VMEM((CONV_HALO + tm, D_BRANCH), F32),
                        pltpu.VMEM((tm, D_BRANCH), F32)],
        compiler_params=pltpu.CompilerParams(
            dimension_semantics=("arbitrary", "arbitrary"), vmem_limit_bytes=VMEM_LIMIT_BYTES),
        name="proj_mix",
    )(x, w_in, sg_w, sg_b_bc, v_ln_g, v_ln_b, conv_w, conv_b, cv_ln_g, cv_ln_b, w_branch)


def _moba_kernel(slopes_ref, q_ref, k_ref, v_ref, pc_ref, o_ref,
                 kmean_ref, qx_ref, m_ref, acc_ref, *, n_blocks):
    hp = pl.program_id(1)
    qb = pl.program_id(2)
    blk = MOBA_BLOCK
    lane = lax.broadcasted_iota(jnp.int32, (blk, LANES), 1)
    rowi = lax.broadcasted_iota(jnp.int32, (blk, LANES), 0)
    head_lanes = (lane < HEAD_DIM, lane >= HEAD_DIM)

    @pl.when(qb == 0)
    def _():
        kmean_ref[...] = jnp.zeros((LANES, LANES), F32)
        for j in range(n_blocks):
            kb = k_ref[j * blk:(j + 1) * blk, :].astype(F32)
            kmean_ref[j:j + 1, :] = jnp.sum(kb, axis=0, keepdims=True) * (1.0 / blk)

    km = kmean_ref[...]
    km_hi = km.astype(BF16)
    km_lo = (km - km_hi.astype(F32)).astype(BF16)
    qp = q_ref[...]
    qbv = jnp.full((blk, LANES), qb, jnp.int32)
    t_lo = rowi.astype(F32)
    t_hi = (qbv * blk).astype(F32)
    past = lane < qbv
    neg_inf = jnp.float32(-jnp.inf)

    for h in range(2):
        slope = slopes_ref[2 * hp + h]
        qh = jnp.where(head_lanes[h], qp, jnp.zeros_like(qp))
        gate = _dot_nt(qh, km_hi) + _dot_nt(qh, km_lo)
        g = jnp.where(past, gate, neg_inf)
        selected = lane == qbv
        for r in range(MOBA_TOPK):
            mx = jnp.max(g, axis=1, keepdims=True)
            idx = jnp.min(jnp.where(g == mx, lane, LANES), axis=1, keepdims=True)
            pick = lane == idx
            selected = jnp.logical_or(selected, jnp.logical_and(pick, qbv > r))
            g = jnp.where(pick, neg_inf, g)
        extra = jnp.where(selected, 0.0, MASK_VALUE)
        extra = jnp.where(lane == _PC_ONE_A, -slope * t_hi, extra)
        extra = jnp.where(lane == _PC_ONE_B, -slope * t_lo, extra)
        extra = jnp.where(jnp.logical_or(lane == _PC_BLK, lane == _PC_POS), slope, extra)
        extra = jnp.where(lane > _PC_POS, 0.0, extra)
        qx_ref[h] = jnp.concatenate([qh, extra.astype(BF16)], axis=1)

    def keys(r0):
        return jnp.concatenate([k_ref[pl.ds(r0, blk), :], pc_ref[pl.ds(r0, blk), :]], axis=1)

    def values(r0, h):
        vp = v_ref[pl.ds(r0, blk), :]
        return jnp.where(head_lanes[h], vp, jnp.ones_like(vp))

    r_own = pl.multiple_of(qb * blk, blk)
    kk = keys(r_own)
    col2 = lax.broadcasted_iota(jnp.int32, (blk, blk), 1)
    row2 = lax.broadcasted_iota(jnp.int32, (blk, blk), 0)
    for h in range(2):
        s = jnp.where(col2 <= row2, _dot_nt(qx_ref[h], kk), MASK_VALUE)
        m = jnp.max(s, axis=1, keepdims=True)
        p = jnp.exp(s - m)
        acc_ref[h] = _dot(p.astype(BF16), values(r_own, h))
        m_ref[h] = jnp.broadcast_to(m, (blk, LANES))

    def past_block(j, carry):
        r0 = pl.multiple_of(j * blk, blk)
        kk = keys(r0)
        for h in range(2):
            s = _dot_nt(qx_ref[h], kk)
            m_old = m_ref[h][:, 0:1]
            m_new = jnp.maximum(m_old, jnp.max(s, axis=1, keepdims=True))
            alpha = jnp.exp(m_old - m_new)
            p = jnp.exp(s - m_new)
            acc_ref[h] = alpha * acc_ref[h] + _dot(p.astype(BF16), values(r0, h))
            m_ref[h] = jnp.broadcast_to(m_new, (blk, LANES))
        return carry

    lax.fori_loop(0, qb, past_block, 0)

    outs = []
    for h in range(2):
        acc = acc_ref[h]
        outs.append(acc / pltpu.roll(acc, HEAD_DIM, axis=1))
    o_ref[...] = jnp.where(head_lanes[0], outs[0], outs[1]).astype(o_ref.dtype)


def _moba(q, k, v, pos_const, slopes):
    B, S, _ = q.shape
    n_blocks = S // MOBA_BLOCK
    n_pairs = N_HEADS * HEAD_DIM // LANES
    grid_spec = pltpu.PrefetchScalarGridSpec(
        num_scalar_prefetch=1,
        grid=(B, n_pairs, n_blocks),
        in_specs=[
            pl.BlockSpec((None, MOBA_BLOCK, LANES), lambda b, p, i, s: (b, i, p)),
            pl.BlockSpec((None, S, LANES), lambda b, p, i, s: (b, 0, p)),
            pl.BlockSpec((None, S, LANES), lambda b, p, i, s: (b, 0, p)),
            pl.BlockSpec((S, LANES), lambda b, p, i, s: (0, 0)),
        ],
        out_specs=pl.BlockSpec((None, MOBA_BLOCK, LANES), lambda b, p, i, s: (b, i, p)),
        scratch_shapes=[
            pltpu.VMEM((LANES, LANES), F32),
            pltpu.VMEM((2, MOBA_BLOCK, 2 * LANES), BF16),
            pltpu.VMEM((2, MOBA_BLOCK, LANES), F32),
            pltpu.VMEM((2, MOBA_BLOCK, LANES), F32),
        ],
    )
    return pl.pallas_call(
        functools.partial(_moba_kernel, n_blocks=n_blocks),
        grid_spec=grid_spec,
        out_shape=jax.ShapeDtypeStruct((B, S, D_BRANCH), BF16),
        compiler_params=pltpu.CompilerParams(
            dimension_semantics=("arbitrary", "arbitrary", "arbitrary"),
            vmem_limit_bytes=VMEM_LIMIT_BYTES),
        name="moba",
    )(slopes, q, k, v, pos_const)


def _merge_out_kernel(x_ref, mixed_ref, gate_c_ref, att_ref, gc_ref, wbc_ref, wout_ref, g_ref, b_ref,
                      o_ref, *, alpha):
    y_c = (att_ref[...].astype(F32) * gc_ref[...].astype(F32)).astype(BF16)
    mixed = mixed_ref[...] + gate_c_ref[...].astype(F32) * _dot(y_c, wbc_ref[...])
    out = _dot(mixed.astype(BF16), wout_ref[...])
    o_ref[...] = _layer_norm(alpha * x_ref[...] + out, g_ref[...], b_ref[...])


def _merge_out(x, mixed_ab, gate_c, att, gc, w_branch, w_out, ln_g, ln_b, *, layer, tm, alpha):
    B, S, D = x.shape
    row_spec = lambda width: pl.BlockSpec((None, tm, width), lambda b, i: (b, i, 0))
    full = lambda shape: pl.BlockSpec(shape, lambda b, i: (0,) * len(shape))
    return pl.pallas_call(
        functools.partial(_merge_out_kernel, alpha=alpha),
        grid=(B, S // tm),
        in_specs=[row_spec(D), row_spec(D_MODEL), row_spec(D_MODEL), row_spec(D_BRANCH), row_spec(D_BRANCH),
                  pl.BlockSpec((None, None, D_BRANCH, D_MODEL), lambda b, i: (layer, 2, 0, 0)),
                  pl.BlockSpec((None, D_MODEL, D_MODEL), lambda b, i: (layer, 0, 0)),
                  full((1, D_MODEL)), full((1, D_MODEL))],
        out_specs=row_spec(D_MODEL),
        out_shape=jax.ShapeDtypeStruct((B, S, D_MODEL), F32),
        compiler_params=pltpu.CompilerParams(
            dimension_semantics=("arbitrary", "arbitrary"), vmem_limit_bytes=VMEM_LIMIT_BYTES),
        name="merge_out",
    )(x, mixed_ab, gate_c, att, gc, w_branch, w_out, ln_g, ln_b)


def _position_constants(S):
    s = jnp.arange(S, dtype=jnp.int32)
    blk, pos = s // MOBA_BLOCK, s % MOBA_BLOCK
    lane = jnp.arange(LANES, dtype=jnp.int32)[None, :]
    pc = (lane == blk[:, None]).astype(F32)
    pc = jnp.where((lane == _PC_ONE_A) | (lane == _PC_ONE_B), 1.0, pc)
    pc = jnp.where(lane == _PC_BLK, (blk * MOBA_BLOCK).astype(F32)[:, None], pc)
    pc = jnp.where(lane == _PC_POS, pos.astype(F32)[:, None], pc)
    return pc.astype(BF16)


def kernel(x, w_in, sg_w, sg_b, v_ln_g, v_ln_b, conv_w, conv_b, cv_ln_g, cv_ln_b, w_branch, w_out, ln_g, ln_b):
    depth = w_in.shape[0]
    B, S, D = x.shape
    assert D == D_MODEL and S % MOBA_BLOCK == 0 and S // MOBA_BLOCK <= _PC_ONE_A
    alpha = (2 * depth) ** 0.25
    tm = 256
    pos_const = _position_constants(S)
    slopes = 2.0 ** (-(jnp.arange(1, N_HEADS + 1, dtype=F32) * 8.0 / N_HEADS))
    w_in_b = w_in.astype(BF16)
    w_br_b = w_branch.astype(BF16)
    w_out_b = w_out.astype(BF16)
    sg_b_bc = jnp.broadcast_to(sg_b[..., None], sg_b.shape + (D_BRANCH // GMLP_GROUPS,))
    row = lambda a, l: a[l][None, :]
    for l in range(depth):
        mixed_ab, gate_c, q, k, v, gc = _proj_mix(
            x, w_in_b, sg_w[l], sg_b_bc[l], row(v_ln_g, l), row(v_ln_b, l), conv_w[l], row(conv_b, l),
            row(cv_ln_g, l), row(cv_ln_b, l), w_br_b, layer=l, tm=tm)
        att = _moba(q, k, v, pos_const, slopes)
        x = _merge_out(x, mixed_ab, gate_c, att, gc, w_br_b, w_out_b, row(ln_g, l), row(ln_b, l),
                       layer=l, tm=tm, alpha=alpha)
    return x
```

```python
import functools

import jax
import jax.numpy as jnp
from jax import lax
from jax.experimental import pallas as pl
from jax.experimental.pallas import tpu as pltpu

F32 = jnp.float32
BF16 = jnp.bfloat16

D_MODEL = 1024
D_BRANCH = 512
GMLP_GROUPS = 4
GMLP_CHUNK = 128
CONV_WIDTH = 31
CONV_HALO = 32
HEAD_DIM = 64
N_HEADS = 8
MOBA_BLOCK = 256
MOBA_TOPK = 3
N_BRANCH = 3
W_IN_COLS = 10 * D_BRANCH + N_BRANCH * D_MODEL
LN_EPS = 1e-5
MASK_VALUE = -1e30
LANES = 128
N_PAIRS = N_HEADS * HEAD_DIM // LANES
MAX_BLOCKS = 32
VMEM_LIMIT_BYTES = 56 * 1024 * 1024

_C_AU, _C_AV, _C_AG = 0, 512, 1024
_C_BVAL, _C_BGLU, _C_BG = 1536, 2048, 2560
_C_Q, _C_K, _C_V, _C_CG = 3072, 3584, 4096, 4608
_C_MERGE = 5120

_PC_ONE_A, _PC_ONE_B, _PC_BLK, _PC_POS = 32, 33, 34, 35


def _sigmoid(x):
    return 1.0 / (1.0 + jnp.exp(-x))


def _silu(x):
    return x * _sigmoid(x)


def _layer_norm(x, g, b):
    mu = jnp.mean(x, axis=-1, keepdims=True)
    xc = x - mu
    var = jnp.mean(xc * xc, axis=-1, keepdims=True)
    return xc * lax.rsqrt(var + LN_EPS) * g + b


def _dot(a, b):
    return jnp.dot(a, b, preferred_element_type=F32)


def _dot_nt(a, b):
    return lax.dot_general(a, b, (((1,), (1,)), ((), ())), preferred_element_type=F32)


def _proj_mix_kernel(x_ref, w_in_ref, wqt_ref, wvt_ref, sg_w_ref, sg_b_ref, vg_ref, vb_ref, cw_ref, cb_ref,
                     cg_ref, cbeta_ref, wbr_ref,
                     mixed_ref, gate_c_ref, qt_ref, k_ref, vt_ref, gc_ref, kmean_ref,
                     conv_buf, conv_out, *, tm):
    i = pl.program_id(1)
    xb = x_ref[...].astype(BF16)

    def proj(c0, width):
        return _dot(xb, w_in_ref[:, c0:c0 + width])

    vn = _layer_norm(proj(_C_AV, D_BRANCH), vg_ref[...], vb_ref[...]).astype(BF16)
    row = lax.broadcasted_iota(jnp.int32, (GMLP_CHUNK, GMLP_CHUNK), 0)
    col = lax.broadcasted_iota(jnp.int32, (GMLP_CHUNK, GMLP_CHUNK), 1)
    cgw = D_BRANCH // GMLP_GROUPS
    z_rows = []
    for c in range(tm // GMLP_CHUNK):
        z_groups = []
        for g in range(GMLP_GROUPS):
            wm = jnp.where(col <= row, sg_w_ref[g], 0.0).astype(BF16)
            vcg = vn[c * GMLP_CHUNK:(c + 1) * GMLP_CHUNK, g * cgw:(g + 1) * cgw]
            z_groups.append(_dot(wm, vcg) + sg_b_ref[g])
        z_rows.append(jnp.concatenate(z_groups, axis=1))
    z = jnp.concatenate(z_rows, axis=0)
    y_a = proj(_C_AU, D_BRANCH) * z * _silu(proj(_C_AG, D_BRANCH))
    mixed = _sigmoid(proj(_C_MERGE, D_MODEL)) * _dot(y_a.astype(BF16), wbr_ref[0])

    @pl.when(i == 0)
    def _():
        conv_buf[0:CONV_HALO, :] = jnp.zeros((CONV_HALO, D_BRANCH), F32)

    conv_buf[CONV_HALO:CONV_HALO + tm, :] = proj(_C_BVAL, D_BRANCH) * _sigmoid(proj(_C_BGLU, D_BRANCH))
    rc = 64
    first = CONV_HALO - (CONV_WIDTH - 1)
    for r in range(tm // rc):
        for lc in range(D_BRANCH // LANES):
            ls = slice(lc * LANES, (lc + 1) * LANES)
            acc = jnp.zeros((rc, LANES), F32)
            for j in range(CONV_WIDTH):
                r0 = first + r * rc + j
                acc = acc + conv_buf[r0:r0 + rc, ls] * cw_ref[j:j + 1, ls]
            conv_out[r * rc:(r + 1) * rc, ls] = acc
    conv = conv_out[...] + cb_ref[...]
    conv_buf[0:CONV_HALO, :] = conv_buf[tm:tm + CONV_HALO, :]
    cv = _silu(_layer_norm(conv, cg_ref[...], cbeta_ref[...]))
    y_b = cv * _silu(proj(_C_BG, D_BRANCH))
    mixed = mixed + _sigmoid(proj(_C_MERGE + D_MODEL, D_MODEL)) * _dot(y_b.astype(BF16), wbr_ref[1])
    mixed_ref[...] = mixed

    gate_c_ref[...] = _sigmoid(proj(_C_MERGE + 2 * D_MODEL, D_MODEL)).astype(BF16)
    qt_ref[...] = (_dot_nt(wqt_ref[...], xb) * (HEAD_DIM ** -0.5)).astype(BF16)
    kf = proj(_C_K, D_BRANCH)
    k_ref[...] = kf.astype(BF16)
    for c in range(tm // MOBA_BLOCK):
        rs = slice(c * MOBA_BLOCK, (c + 1) * MOBA_BLOCK)
        vt_ref[c] = _dot_nt(wvt_ref[...], xb[rs, :]).astype(BF16)
        kmean_ref[c] = jnp.mean(kf[rs, :], axis=0, keepdims=True)
    gc_ref[...] = _silu(proj(_C_CG, D_BRANCH)).astype(BF16)


def _proj_mix(x, w_in, w_qt, w_vt, sg_w, sg_b_bc, v_ln_g, v_ln_b, conv_w, conv_b, cv_ln_g, cv_ln_b, w_branch,
              *, layer, tm):
    B, S, D = x.shape
    nb = S // MOBA_BLOCK
    bpt = tm // MOBA_BLOCK
    grid = (B, S // tm)
    row_spec = lambda width: pl.BlockSpec((None, tm, width), lambda b, i: (b, i, 0))
    full = lambda shape: pl.BlockSpec(shape, lambda b, i: (0,) * len(shape))
    layer_spec = lambda shape: pl.BlockSpec((None,) + shape, lambda b, i: (layer,) + (0,) * len(shape))
    vec = full((1, D_BRANCH))
    return pl.pallas_call(
        functools.partial(_proj_mix_kernel, tm=tm),
        grid=grid,
        in_specs=[
            row_spec(D),
            pl.BlockSpec((None, D, W_IN_COLS), lambda b, i: (layer, 0, 0), pipeline_mode=pl.Buffered(1)),
            layer_spec((D_BRANCH, D)), layer_spec((D_BRANCH, D)),
            full((GMLP_GROUPS, GMLP_CHUNK, GMLP_CHUNK)),
            full((GMLP_GROUPS, GMLP_CHUNK, GMLP_CHUNK)),
            vec, vec,
            full((CONV_WIDTH, D_BRANCH)),
            vec, vec, vec,
            pl.BlockSpec((None, 2, D_BRANCH, D_MODEL), lambda b, i: (layer, 0, 0, 0)),
        ],
        out_specs=[row_spec(D_MODEL), row_spec(D_MODEL),
                   pl.BlockSpec((None, D_BRANCH, tm), lambda b, i: (b, 0, i)),
                   row_spec(D_BRANCH),
                   pl.BlockSpec((None, bpt, D_BRANCH, MOBA_BLOCK), lambda b, i: (b, i, 0, 0)),
                   row_spec(D_BRANCH),
                   pl.BlockSpec((None, bpt, 1, D_BRANCH), lambda b, i: (b, i, 0, 0))],
        out_shape=[
            jax.ShapeDtypeStruct((B, S, D_MODEL), F32),
            jax.ShapeDtypeStruct((B, S, D_MODEL), BF16),
            jax.ShapeDtypeStruct((B, D_BRANCH, S), BF16),
            jax.ShapeDtypeStruct((B, S, D_BRANCH), BF16),
            jax.ShapeDtypeStruct((B, nb, D_BRANCH, MOBA_BLOCK), BF16),
            jax.ShapeDtypeStruct((B, S, D_BRANCH), BF16),
            jax.ShapeDtypeStruct((B, nb, 1, D_BRANCH), F32),
        ],
        scratch_shapes=[pltpu.VMEM((CONV_HALO + tm, D_BRANCH), F32),
                        pltpu.VMEM((tm, D_BRANCH), F32)],
        compiler_params=pltpu.CompilerParams(
            dimension_semantics=("arbitrary", "arbitrary"), vmem_limit_bytes=VMEM_LIMIT_BYTES),
        name="proj_mix",
    )(x, w_in, w_qt, w_vt, sg_w, sg_b_bc, v_ln_g, v_ln_b, conv_w, conv_b, cv_ln_g, cv_ln_b, w_branch)


def _moba_kernel(qt_ref, k_ref, vt_ref, pc_ref, km_ref, o_ref, qx_ref, s_ref, m_ref, acc_ref):
    qb = pl.program_id(1)
    blk = MOBA_BLOCK
    nbm = MAX_BLOCKS
    rown = lax.broadcasted_iota(jnp.int32, (nbm, blk), 0)
    row_f = lax.broadcasted_iota(jnp.int32, (LANES, blk), 0)
    head_rows = (row_f < HEAD_DIM, row_f >= HEAD_DIM)
    qbv = jnp.full((nbm, blk), qb, jnp.int32)
    row8 = lax.broadcasted_iota(jnp.int32, (8, blk), 0)
    t_lo = lax.broadcasted_iota(jnp.int32, (8, blk), 1).astype(F32)
    t_hi = (qbv[0:8, :] * blk).astype(F32)
    neg_inf = jnp.float32(-jnp.inf)
    zeros_tail = jnp.zeros((LANES - nbm - 8, blk), F32)

    for p in range(N_PAIRS):
        qtp = qt_ref[p * LANES:(p + 1) * LANES, :]
        km = km_ref[:, p * LANES:(p + 1) * LANES]
        km_hi = km.astype(BF16)
        km_lo = (km - km_hi.astype(F32)).astype(BF16)
        for h in range(2):
            slope = 2.0 ** (-(2 * p + h + 1) * 8.0 / N_HEADS)
            qh = jnp.where(head_rows[h], qtp, jnp.zeros_like(qtp))
            gate = _dot(km_hi, qh) + _dot(km_lo, qh)
            g = jnp.where(rown < qbv, gate, neg_inf)
            selected = rown < 0
            for r in range(MOBA_TOPK):
                mx = jnp.max(g, axis=0, keepdims=True)
                idx = jnp.min(jnp.where(g == mx, rown, nbm), axis=0, keepdims=True)
                pick = rown == idx
                selected = jnp.logical_or(selected, jnp.logical_and(pick, qbv > r))
                g = jnp.where(pick, neg_inf, g)
            mask_rows = jnp.where(selected, 0.0, MASK_VALUE)
            pos_rows = jnp.where(row8 == 0, -slope * t_hi,
                                 jnp.where(row8 == 1, -slope * t_lo, jnp.where(row8 < 4, slope, 0.0)))
            extra = jnp.concatenate([mask_rows, pos_rows, zeros_tail], axis=0)
            qx_ref[2 * p + h] = jnp.concatenate([qh, extra.astype(BF16)], axis=0)

    def values(vt_blk, p, h):
        vtp = vt_blk[p * LANES:(p + 1) * LANES, :]
        return jnp.where(head_rows[h], vtp, jnp.ones_like(vtp))

    lane_pc = lax.broadcasted_iota(jnp.int32, (blk, LANES), 1)

    def block_rows(j):
        return pl.ds(j * blk if isinstance(j, int) else pl.multiple_of(j * blk, blk), blk)

    def keys(j, p, pc_j):
        return jnp.concatenate([k_ref[block_rows(j), p * LANES:(p + 1) * LANES], pc_j], axis=1)

    def pos_tile(j):
        pc_j = pc_ref[block_rows(j), :]
        own = jnp.full((blk, LANES), j, jnp.int32) == jnp.full((blk, LANES), qb, jnp.int32)
        return jnp.where(jnp.logical_and(own, lane_pc < nbm), jnp.zeros_like(pc_j), pc_j)

    def update(hd, s, vt_blk):
        m_old = m_ref[hd]
        m_new = jnp.maximum(m_old, jnp.max(s, axis=0, keepdims=True))
        alpha = jnp.exp(m_old - m_new)
        pexp = jnp.exp(s - m_new)
        acc_ref[hd] = alpha * acc_ref[hd] + _dot(values(vt_blk, hd // 2, hd % 2), pexp.astype(BF16))
        m_ref[hd] = m_new

    m_ref[...] = jnp.full(m_ref.shape, -jnp.inf, F32)
    acc_ref[...] = jnp.zeros(acc_ref.shape, F32)
    pc_0 = pos_tile(0)
    for hd in range(N_HEADS):
        s_ref[hd] = _dot(keys(0, hd // 2, pc_0), qx_ref[hd])

    def past_block(j, carry):
        pc_n = pos_tile(j + 1)
        vt_j = vt_ref[j]
        for hd in range(N_HEADS):
            update(hd, s_ref[hd], vt_j)
            s_ref[hd] = _dot(keys(j + 1, hd // 2, pc_n), qx_ref[hd])
        return carry

    lax.fori_loop(0, qb, past_block, 0)

    key_i = lax.broadcasted_iota(jnp.int32, (blk, blk), 0)
    qry_i = lax.broadcasted_iota(jnp.int32, (blk, blk), 1)
    vt_own = vt_ref[qb]
    for hd in range(N_HEADS):
        update(hd, jnp.where(key_i <= qry_i, s_ref[hd], MASK_VALUE), vt_own)

    outs = []
    for hd in range(N_HEADS):
        acc = acc_ref[hd]
        lo, hi = acc[0:HEAD_DIM, :], acc[HEAD_DIM:LANES, :]
        outs.append(lo / hi if hd % 2 == 0 else hi / lo)
    o_ref[...] = jnp.concatenate(outs, axis=0).T.astype(o_ref.dtype)


def _moba(qt, k, vt, kmean, pos_const):
    B, S, _ = k.shape
    nb = S // MOBA_BLOCK
    return pl.pallas_call(
        _moba_kernel,
        grid=(B, nb),
        in_specs=[
            pl.BlockSpec((None, D_BRANCH, MOBA_BLOCK), lambda b, i: (b, 0, i)),
            pl.BlockSpec((None, S, D_BRANCH), lambda b, i: (b, 0, 0)),
            pl.BlockSpec((None, nb, D_BRANCH, MOBA_BLOCK), lambda b, i: (b, 0, 0, 0)),
            pl.BlockSpec((S, LANES), lambda b, i: (0, 0)),
            pl.BlockSpec((None, MAX_BLOCKS, D_BRANCH), lambda b, i: (b, 0, 0)),
        ],
        out_specs=pl.BlockSpec((None, MOBA_BLOCK, D_BRANCH), lambda b, i: (b, i, 0)),
        out_shape=jax.ShapeDtypeStruct((B, S, D_BRANCH), BF16),
        scratch_shapes=[
            pltpu.VMEM((N_HEADS, 2 * LANES, MOBA_BLOCK), BF16),
            pltpu.VMEM((N_HEADS, MOBA_BLOCK, MOBA_BLOCK), F32),
            pltpu.VMEM((N_HEADS, 1, MOBA_BLOCK), F32),
            pltpu.VMEM((N_HEADS, LANES, MOBA_BLOCK), F32),
        ],
        compiler_params=pltpu.CompilerParams(
            dimension_semantics=("arbitrary", "arbitrary"), vmem_limit_bytes=VMEM_LIMIT_BYTES),
        name="moba",
    )(qt, k, vt, pos_const, kmean)


def _merge_out_kernel(x_ref, mixed_ref, gate_c_ref, att_ref, gc_ref, wbc_ref, wout_ref, g_ref, b_ref,
                      o_ref, *, alpha):
    y_c = (att_ref[...].astype(F32) * gc_ref[...].astype(F32)).astype(BF16)
    mixed = mixed_ref[...] + gate_c_ref[...].astype(F32) * _dot(y_c, wbc_ref[...])
    out = _dot(mixed.astype(BF16), wout_ref[...])
    o_ref[...] = _layer_norm(alpha * x_ref[...] + out, g_ref[...], b_ref[...])


def _merge_out(x, mixed_ab, gate_c, att, gc, w_branch, w_out, ln_g, ln_b, *, layer, tm, alpha):
    B, S, D = x.shape
    row_spec = lambda width: pl.BlockSpec((None, tm, width), lambda b, i: (b, i, 0))
    full = lambda shape: pl.BlockSpec(shape, lambda b, i: (0,) * len(shape))
    return pl.pallas_call(
        functools.partial(_merge_out_kernel, alpha=alpha),
        grid=(B, S // tm),
        in_specs=[row_spec(D), row_spec(D_MODEL), row_spec(D_MODEL), row_spec(D_BRANCH), row_spec(D_BRANCH),
                  pl.BlockSpec((None, None, D_BRANCH, D_MODEL), lambda b, i: (layer, 2, 0, 0)),
                  pl.BlockSpec((None, D_MODEL, D_MODEL), lambda b, i: (layer, 0, 0)),
                  full((1, D_MODEL)), full((1, D_MODEL))],
        out_specs=row_spec(D_MODEL),
        out_shape=jax.ShapeDtypeStruct((B, S, D_MODEL), F32),
        compiler_params=pltpu.CompilerParams(
            dimension_semantics=("arbitrary", "arbitrary"), vmem_limit_bytes=VMEM_LIMIT_BYTES),
        name="merge_out",
    )(x, mixed_ab, gate_c, att, gc, w_branch, w_out, ln_g, ln_b)


def _position_constants(S):
    s = jnp.arange(S, dtype=jnp.int32)
    blk, pos = s // MOBA_BLOCK, s % MOBA_BLOCK
    lane = jnp.arange(LANES, dtype=jnp.int32)[None, :]
    pc = (lane == blk[:, None]).astype(F32)
    pc = jnp.where((lane == _PC_ONE_A) | (lane == _PC_ONE_B), 1.0, pc)
    pc = jnp.where(lane == _PC_BLK, (blk * MOBA_BLOCK).astype(F32)[:, None], pc)
    pc = jnp.where(lane == _PC_POS, pos.astype(F32)[:, None], pc)
    return pc.astype(BF16)


def kernel(x, w_in, sg_w, sg_b, v_ln_g, v_ln_b, conv_w, conv_b, cv_ln_g, cv_ln_b, w_branch, w_out, ln_g, ln_b):
    depth = w_in.shape[0]
    B, S, D = x.shape
    nb = S // MOBA_BLOCK
    assert D == D_MODEL and S % MOBA_BLOCK == 0 and nb <= MAX_BLOCKS
    alpha = (2 * depth) ** 0.25
    tm = 256
    pos_const = _position_constants(S)
    w_in_b = w_in.astype(BF16)
    w_qt = jnp.swapaxes(w_in_b[:, :, _C_Q:_C_Q + D_BRANCH], 1, 2)
    w_vt = jnp.swapaxes(w_in_b[:, :, _C_V:_C_V + D_BRANCH], 1, 2)
    w_br_b = w_branch.astype(BF16)
    w_out_b = w_out.astype(BF16)
    sg_b_bc = jnp.broadcast_to(sg_b[..., None], sg_b.shape + (D_BRANCH // GMLP_GROUPS,))
    row = lambda a, l: a[l][None, :]
    for l in range(depth):
        mixed_ab, gate_c, qt, k, vt, gc, kmean = _proj_mix(
            x, w_in_b, w_qt, w_vt, sg_w[l], sg_b_bc[l], row(v_ln_g, l), row(v_ln_b, l), conv_w[l], row(conv_b, l),
            row(cv_ln_g, l), row(cv_ln_b, l), w_br_b, layer=l, tm=tm)
        kmean = jnp.pad(kmean.reshape(B, nb, D_BRANCH), ((0, 0), (0, MAX_BLOCKS - nb), (0, 0)))
        att = _moba(qt, k, vt, kmean, pos_const)
        x = _merge_out(x, mixed_ab, gate_c, att, gc, w_br_b, w_out_b, row(ln_g, l), row(ln_b, l),
                       layer=l, tm=tm, alpha=alpha)
    return x
```

```python
import functools

import numpy as np
import jax
import jax.numpy as jnp
from jax import lax
from jax.experimental import pallas as pl
from jax.experimental.pallas import tpu as pltpu

F32 = jnp.float32
BF16 = jnp.bfloat16

D_MODEL = 1024
D_BRANCH = 512
GMLP_GROUPS = 4
GMLP_CHUNK = 128
CONV_WIDTH = 31
CONV_HALO = 32
CONV_CHUNK = 128
HEAD_DIM = 64
N_HEADS = 8
MOBA_BLOCK = 256
MOBA_TOPK = 3
N_BRANCH = 3
W_IN_COLS = 10 * D_BRANCH + N_BRANCH * D_MODEL
LN_EPS = 1e-5
MASK_VALUE = -1e30
LANES = 128
SUBLANES = 8
LOG2E = 1.4426950408889634
N_PAIRS = N_HEADS * HEAD_DIM // LANES
MAX_BLOCKS = 32
VMEM_LIMIT_BYTES = 56 * 1024 * 1024

_C_AU, _C_AV, _C_AG = 0, 512, 1024
_C_BVAL, _C_BGLU, _C_BG = 1536, 2048, 2560
_C_Q, _C_K, _C_V, _C_CG = 3072, 3584, 4096, 4608
_C_MERGE = 5120

_PC_ONE, _PC_BLK, _PC_POS = 32, 35, 38
_PC_ROWS = 16
_SUM_ROWS = 16


def _sigmoid(x):
    return 1.0 / (1.0 + jnp.exp(-x))


def _silu(x):
    return x * _sigmoid(x)


def _layer_norm(x, g, b):
    mu = jnp.mean(x, axis=-1, keepdims=True)
    xc = x - mu
    var = jnp.mean(xc * xc, axis=-1, keepdims=True)
    return xc * lax.rsqrt(var + LN_EPS) * g + b


def _dot(a, b):
    return jnp.dot(a, b, preferred_element_type=F32)


def _proj_mix_kernel(x_ref, w_in_ref, sg_w_ref, sg_b_ref, vg_ref, vb_ref, cw_ref, cb_ref,
                     cg_ref, cbeta_ref, wbr_ref,
                     mixed_ref, gate_c_ref, qt_ref, k_ref, vt_ref, gc_ref, kmean_ref,
                     conv_buf, conv_out, shift_buf, *, tm):
    i = pl.program_id(1)
    xb = x_ref[...].astype(BF16)

    def proj(c0, width):
        return _dot(xb, w_in_ref[:, c0:c0 + width])

    vn = _layer_norm(proj(_C_AV, D_BRANCH), vg_ref[...], vb_ref[...]).astype(BF16)
    row = lax.broadcasted_iota(jnp.int32, (GMLP_CHUNK, GMLP_CHUNK), 0)
    col = lax.broadcasted_iota(jnp.int32, (GMLP_CHUNK, GMLP_CHUNK), 1)
    cgw = D_BRANCH // GMLP_GROUPS
    z_rows = []
    for c in range(tm // GMLP_CHUNK):
        z_groups = []
        for g in range(GMLP_GROUPS):
            wm = jnp.where(col <= row, sg_w_ref[g], 0.0).astype(BF16)
            vcg = vn[c * GMLP_CHUNK:(c + 1) * GMLP_CHUNK, g * cgw:(g + 1) * cgw]
            z_groups.append(_dot(wm, vcg) + sg_b_ref[g])
        z_rows.append(jnp.concatenate(z_groups, axis=1))
    z = jnp.concatenate(z_rows, axis=0)
    y_a = proj(_C_AU, D_BRANCH) * z * _silu(proj(_C_AG, D_BRANCH))
    mixed_ref[...] = _sigmoid(proj(_C_MERGE, D_MODEL)) * _dot(y_a.astype(BF16), wbr_ref[0])

    @pl.when(i == 0)
    def _():
        conv_buf[0:CONV_HALO, :] = jnp.zeros((CONV_HALO, D_BRANCH), F32)

    conv_buf[CONV_HALO:CONV_HALO + tm, :] = proj(_C_BVAL, D_BRANCH) * _sigmoid(proj(_C_BGLU, D_BRANCH))
    rc = CONV_CHUNK
    first = CONV_HALO - (CONV_WIDTH - 1)
    for r in range(tm // rc):
        for lc in range(D_BRANCH // LANES):
            ls = slice(lc * LANES, (lc + 1) * LANES)
            acc = jnp.zeros((rc, LANES), F32)
            for phase in range(SUBLANES):
                taps = [j for j in range(CONV_WIDTH) if (first + j) % SUBLANES == phase]
                if not taps:
                    continue
                offs = [(first + j) // SUBLANES * SUBLANES for j in taps]
                r0 = r * rc + phase
                n = rc + max(offs)
                shift_buf[phase, 0:n, :] = conv_buf[r0:r0 + n, ls]
                for j, a in zip(taps, offs):
                    acc = acc + shift_buf[phase, a:a + rc, :] * cw_ref[j:j + 1, ls]
            conv_out[r * rc:(r + 1) * rc, ls] = acc
    conv = conv_out[...] + cb_ref[...]
    conv_buf[0:CONV_HALO, :] = conv_buf[tm:tm + CONV_HALO, :]
    cv = _silu(_layer_norm(conv, cg_ref[...], cbeta_ref[...]))
    y_b = cv * _silu(proj(_C_BG, D_BRANCH))
    mixed_ref[...] += _sigmoid(proj(_C_MERGE + D_MODEL, D_MODEL)) * _dot(y_b.astype(BF16), wbr_ref[1])

    gate_c_ref[...] = _sigmoid(proj(_C_MERGE + 2 * D_MODEL, D_MODEL)).astype(BF16)
    qt_ref[...] = (proj(_C_Q, D_BRANCH) * (HEAD_DIM ** -0.5 * LOG2E)).T.astype(BF16)
    kf = proj(_C_K, D_BRANCH)
    k_ref[...] = kf.astype(BF16)
    vf = proj(_C_V, D_BRANCH)
    for c in range(tm // MOBA_BLOCK):
        rs = slice(c * MOBA_BLOCK, (c + 1) * MOBA_BLOCK)
        vt_ref[c] = vf[rs, :].T.astype(BF16)
        kmean_ref[c] = jnp.mean(kf[rs, :], axis=0, keepdims=True)
    gc_ref[...] = _silu(proj(_C_CG, D_BRANCH)).astype(BF16)


def _proj_mix(x, w_in, sg_w, sg_b_bc, v_ln_g, v_ln_b, conv_w, conv_b, cv_ln_g, cv_ln_b, w_branch,
              *, layer, tm):
    B, S, D = x.shape
    nb = S // MOBA_BLOCK
    bpt = tm // MOBA_BLOCK
    grid = (B, S // tm)
    row_spec = lambda width: pl.BlockSpec((None, tm, width), lambda b, i: (b, i, 0))
    full = lambda shape: pl.BlockSpec(shape, lambda b, i: (0,) * len(shape))
    vec = full((1, D_BRANCH))
    return pl.pallas_call(
        functools.partial(_proj_mix_kernel, tm=tm),
        grid=grid,
        in_specs=[
            row_spec(D),
            pl.BlockSpec((None, D, W_IN_COLS), lambda b, i: (layer, 0, 0), pipeline_mode=pl.Buffered(1)),
            full((GMLP_GROUPS, GMLP_CHUNK, GMLP_CHUNK)),
            full((GMLP_GROUPS, GMLP_CHUNK, GMLP_CHUNK)),
            vec, vec,
            full((CONV_WIDTH, D_BRANCH)),
            vec, vec, vec,
            pl.BlockSpec((None, 2, D_BRANCH, D_MODEL), lambda b, i: (layer, 0, 0, 0)),
        ],
        out_specs=[row_spec(D_MODEL), row_spec(D_MODEL),
                   pl.BlockSpec((None, D_BRANCH, tm), lambda b, i: (b, 0, i)),
                   row_spec(D_BRANCH),
                   pl.BlockSpec((None, bpt, D_BRANCH, MOBA_BLOCK), lambda b, i: (b, i, 0, 0)),
                   row_spec(D_BRANCH),
                   pl.BlockSpec((None, bpt, 1, D_BRANCH), lambda b, i: (b, i, 0, 0))],
        out_shape=[
            jax.ShapeDtypeStruct((B, S, D_MODEL), F32),
            jax.ShapeDtypeStruct((B, S, D_MODEL), BF16),
            jax.ShapeDtypeStruct((B, D_BRANCH, S), BF16),
            jax.ShapeDtypeStruct((B, S, D_BRANCH), BF16),
            jax.ShapeDtypeStruct((B, nb, D_BRANCH, MOBA_BLOCK), BF16),
            jax.ShapeDtypeStruct((B, S, D_BRANCH), BF16),
            jax.ShapeDtypeStruct((B, nb, 1, D_BRANCH), F32),
        ],
        scratch_shapes=[pltpu.VMEM((CONV_HALO + tm, D_BRANCH), F32),
                        pltpu.VMEM((tm, D_BRANCH), F32),
                        pltpu.VMEM((SUBLANES, CONV_CHUNK + CONV_HALO, LANES), F32)],
        compiler_params=pltpu.CompilerParams(
            dimension_semantics=("arbitrary", "arbitrary"), vmem_limit_bytes=VMEM_LIMIT_BYTES),
        name="proj_mix",
    )(x, w_in, sg_w, sg_b_bc, v_ln_g, v_ln_b, conv_w, conv_b, cv_ln_g, cv_ln_b, w_branch)


def _split_bf16(c, parts=3):
    out, rest = [], np.float32(c)
    for _ in range(parts):
        piece = np.float32(rest.astype(BF16))
        out.append(float(piece))
        rest = np.float32(rest - piece)
    return out


def _moba_kernel(qt_ref, k_ref, vt_ref, pc_ref, km_ref, o_ref, qx_ref, s_ref, m_ref, acc_ref):
    qb = pl.program_id(1)
    blk = MOBA_BLOCK
    nbm = MAX_BLOCKS
    rown = lax.broadcasted_iota(jnp.int32, (nbm, blk), 0)
    row_f = lax.broadcasted_iota(jnp.int32, (LANES, blk), 0)
    head_rows = (row_f < HEAD_DIM, row_f >= HEAD_DIM)
    qbv = jnp.full((nbm, blk), qb, jnp.int32)
    rowp = lax.broadcasted_iota(jnp.int32, (_PC_ROWS, blk), 0) + nbm
    t_pos = (qbv[0:_PC_ROWS, :] * blk + lax.broadcasted_iota(jnp.int32, (_PC_ROWS, blk), 1)).astype(F32)
    neg_inf = jnp.float32(-jnp.inf)
    zeros_tail = jnp.zeros((LANES - nbm - _PC_ROWS, blk), F32)
    ones_rows = jnp.ones((_SUM_ROWS, blk), BF16)

    for p in range(N_PAIRS):
        qtp = qt_ref[p * LANES:(p + 1) * LANES, :]
        km = km_ref[:, p * LANES:(p + 1) * LANES]
        km_hi = km.astype(BF16)
        km_lo = (km - km_hi.astype(F32)).astype(BF16)
        for h in range(2):
            c_parts = _split_bf16(2.0 ** (-(2 * p + h + 1) * 8.0 / N_HEADS) * LOG2E)
            qh = jnp.where(head_rows[h], qtp, jnp.zeros_like(qtp))
            gate = _dot(km_hi, qh) + _dot(km_lo, qh)
            g = jnp.where(rown < qbv, gate, neg_inf)
            selected = rown < 0
            for r in range(MOBA_TOPK):
                mx = jnp.max(g, axis=0, keepdims=True)
                idx = jnp.min(jnp.where(g == mx, rown, nbm), axis=0, keepdims=True)
                pick = rown == idx
                selected = jnp.logical_or(selected, jnp.logical_and(pick, qbv > r))
                g = jnp.where(pick, neg_inf, g)
            mask_rows = jnp.where(selected, 0.0, MASK_VALUE)
            bias_t = -sum(c_parts) * t_pos
            bias_t1 = bias_t.astype(BF16).astype(F32)
            bias_t2 = (bias_t - bias_t1).astype(BF16).astype(F32)
            pos_rows = jnp.zeros((_PC_ROWS, blk), F32)
            for i, piece in enumerate((bias_t1, bias_t2, bias_t - bias_t1 - bias_t2)):
                pos_rows = jnp.where(rowp == _PC_ONE + i, piece, pos_rows)
            for i, c_i in enumerate(c_parts):
                pos_rows = jnp.where(jnp.logical_or(rowp == _PC_BLK + i, rowp == _PC_POS + i), c_i, pos_rows)
            extra = jnp.concatenate([mask_rows, pos_rows, zeros_tail], axis=0)
            qx_ref[2 * p + h] = jnp.concatenate([qh, extra.astype(BF16)], axis=0)

    def values(vt_blk, hd):
        return jnp.concatenate([vt_blk[hd * HEAD_DIM:(hd + 1) * HEAD_DIM, :], ones_rows], axis=0)

    lane_pc = lax.broadcasted_iota(jnp.int32, (blk, LANES), 1)
    n_pairs_slots = (qb + 1) // 2
    n_slots = 2 * n_pairs_slots

    def block_rows(j):
        return pl.ds(j * blk if isinstance(j, int) else pl.multiple_of(j * blk, blk), blk)

    def pos_tile(j, final):
        pc_j = pc_ref[block_rows(j), :]
        if final is False:
            return pc_j
        drop = jnp.logical_and(jnp.full((blk, LANES), final, jnp.int32) != 0, lane_pc < nbm)
        return jnp.where(drop, jnp.zeros_like(pc_j), pc_j)

    def update(hd, s, vt_blk):
        m_old = m_ref[hd]
        m_new = jnp.maximum(m_old, jnp.max(s, axis=0, keepdims=True))
        alpha = jnp.exp2(m_old - m_new)
        pexp = jnp.exp2(s - m_new)
        acc_ref[hd] = alpha * acc_ref[hd] + _dot(values(vt_blk, hd), pexp.astype(BF16))
        m_ref[hd] = m_new

    def phase(src, dst, vt_blk, j_next, final_next):
        lead = 2
        pc_n = pos_tile(j_next, final_next)
        for step in range(N_HEADS + lead):
            if step < N_HEADS:
                kk = jnp.concatenate([k_ref[block_rows(j_next), (step // 2) * LANES:(step // 2 + 1) * LANES], pc_n],
                                     axis=1)
                s_ref[dst, step] = _dot(kk, qx_ref[step])
            if step >= lead:
                update(step - lead, s_ref[src, step - lead], vt_blk)

    m_ref[...] = jnp.full(m_ref.shape, -jnp.inf, F32)
    acc_ref[...] = jnp.zeros(acc_ref.shape, F32)
    pc_0 = pos_tile(0, (qb == 0).astype(jnp.int32))
    for hd in range(N_HEADS):
        kk = jnp.concatenate([k_ref[block_rows(0), (hd // 2) * LANES:(hd // 2 + 1) * LANES], pc_0], axis=1)
        s_ref[0, hd] = _dot(kk, qx_ref[hd])

    def slot_pair(i, carry):
        t0 = 2 * i
        phase(0, 1, vt_ref[t0], t0 + 1, False)
        t2 = jnp.minimum(t0 + 2, qb)
        phase(1, 0, vt_ref[t0 + 1], t2, (t0 + 2 == n_slots).astype(jnp.int32))
        return carry

    lax.fori_loop(0, n_pairs_slots, slot_pair, 0)

    key_i = lax.broadcasted_iota(jnp.int32, (blk, blk), 0)
    qry_i = lax.broadcasted_iota(jnp.int32, (blk, blk), 1)
    vt_own = vt_ref[qb]
    for hd in range(N_HEADS):
        update(hd, jnp.where(key_i <= qry_i, s_ref[0, hd], MASK_VALUE), vt_own)

    outs = []
    for hd in range(N_HEADS):
        acc = acc_ref[hd]
        outs.append(acc[0:HEAD_DIM, :] / acc[HEAD_DIM:HEAD_DIM + 1, :])
    o_ref[...] = jnp.concatenate(outs, axis=0).T.astype(o_ref.dtype)


def _moba(qt, k, vt, kmean, pos_const):
    B, S, _ = k.shape
    nb = S // MOBA_BLOCK
    return pl.pallas_call(
        _moba_kernel,
        grid=(B, nb),
        in_specs=[
            pl.BlockSpec((None, D_BRANCH, MOBA_BLOCK), lambda b, i: (b, 0, i)),
            pl.BlockSpec((None, S, D_BRANCH), lambda b, i: (b, 0, 0)),
            pl.BlockSpec((None, nb, D_BRANCH, MOBA_BLOCK), lambda b, i: (b, 0, 0, 0)),
            pl.BlockSpec((S, LANES), lambda b, i: (0, 0)),
            pl.BlockSpec((None, MAX_BLOCKS, D_BRANCH), lambda b, i: (b, 0, 0)),
        ],
        out_specs=pl.BlockSpec((None, MOBA_BLOCK, D_BRANCH), lambda b, i: (b, i, 0)),
        out_shape=jax.ShapeDtypeStruct((B, S, D_BRANCH), BF16),
        scratch_shapes=[
            pltpu.VMEM((N_HEADS, 2 * LANES, MOBA_BLOCK), BF16),
            pltpu.VMEM((2, N_HEADS, MOBA_BLOCK, MOBA_BLOCK), F32),
            pltpu.VMEM((N_HEADS, 1, MOBA_BLOCK), F32),
            pltpu.VMEM((N_HEADS, HEAD_DIM + _SUM_ROWS, MOBA_BLOCK), F32),
        ],
        compiler_params=pltpu.CompilerParams(
            dimension_semantics=("arbitrary", "arbitrary"), vmem_limit_bytes=VMEM_LIMIT_BYTES),
        name="moba",
    )(qt, k, vt, pos_const, kmean)


def _merge_out_kernel(x_ref, mixed_ref, gate_c_ref, att_ref, gc_ref, wbc_ref, wout_ref, g_ref, b_ref,
                      o_ref, *, alpha):
    y_c = (att_ref[...].astype(F32) * gc_ref[...].astype(F32)).astype(BF16)
    mixed = mixed_ref[...] + gate_c_ref[...].astype(F32) * _dot(y_c, wbc_ref[...])
    out = _dot(mixed.astype(BF16), wout_ref[...])
    o_ref[...] = _layer_norm(alpha * x_ref[...] + out, g_ref[...], b_ref[...])


def _merge_out(x, mixed_ab, gate_c, att, gc, w_branch, w_out, ln_g, ln_b, *, layer, tm, alpha):
    B, S, D = x.shape
    row_spec = lambda width: pl.BlockSpec((None, tm, width), lambda b, i: (b, i, 0))
    full = lambda shape: pl.BlockSpec(shape, lambda b, i: (0,) * len(shape))
    return pl.pallas_call(
        functools.partial(_merge_out_kernel, alpha=alpha),
        grid=(B, S // tm),
        in_specs=[row_spec(D), row_spec(D_MODEL), row_spec(D_MODEL), row_spec(D_BRANCH), row_spec(D_BRANCH),
                  pl.BlockSpec((None, None, D_BRANCH, D_MODEL), lambda b, i: (layer, 2, 0, 0)),
                  pl.BlockSpec((None, D_MODEL, D_MODEL), lambda b, i: (layer, 0, 0)),
                  full((1, D_MODEL)), full((1, D_MODEL))],
        out_specs=row_spec(D_MODEL),
        out_shape=jax.ShapeDtypeStruct((B, S, D_MODEL), F32),
        compiler_params=pltpu.CompilerParams(
            dimension_semantics=("arbitrary", "arbitrary"), vmem_limit_bytes=VMEM_LIMIT_BYTES),
        name="merge_out",
    )(x, mixed_ab, gate_c, att, gc, w_branch, w_out, ln_g, ln_b)


def _position_constants(S):
    s = jnp.arange(S, dtype=jnp.int32)
    blk, pos = s // MOBA_BLOCK, s % MOBA_BLOCK
    lane = jnp.arange(LANES, dtype=jnp.int32)[None, :]
    pc = (lane == blk[:, None]).astype(F32)
    pc = jnp.where((lane >= _PC_ONE) & (lane < _PC_ONE + 3), 1.0, pc)
    pc = jnp.where((lane >= _PC_BLK) & (lane < _PC_BLK + 3), (blk * MOBA_BLOCK).astype(F32)[:, None], pc)
    pc = jnp.where((lane >= _PC_POS) & (lane < _PC_POS + 3), pos.astype(F32)[:, None], pc)
    return pc.astype(BF16)


def kernel(x, w_in, sg_w, sg_b, v_ln_g, v_ln_b, conv_w, conv_b, cv_ln_g, cv_ln_b, w_branch, w_out, ln_g, ln_b):
    depth = w_in.shape[0]
    B, S, D = x.shape
    nb = S // MOBA_BLOCK
    assert D == D_MODEL and S % MOBA_BLOCK == 0 and nb <= MAX_BLOCKS == _PC_ONE
    alpha = (2 * depth) ** 0.25
    tm = 256
    pos_const = _position_constants(S)
    w_in_b = w_in.astype(BF16)
    w_br_b = w_branch.astype(BF16)
    w_out_b = w_out.astype(BF16)
    sg_b_bc = jnp.broadcast_to(sg_b[..., None], sg_b.shape + (D_BRANCH // GMLP_GROUPS,))
    row = lambda a, l: a[l][None, :]
    for l in range(depth):
        mixed_ab, gate_c, qt, k, vt, gc, kmean = _proj_mix(
            x, w_in_b, sg_w[l], sg_b_bc[l], row(v_ln_g, l), row(v_ln_b, l), conv_w[l], row(conv_b, l),
            row(cv_ln_g, l), row(cv_ln_b, l), w_br_b, layer=l, tm=tm)
        kmean = jnp.pad(kmean.reshape(B, nb, D_BRANCH), ((0, 0), (0, MAX_BLOCKS - nb), (0, 0)))
        att = _moba(qt, k, vt, kmean, pos_const)
        x = _merge_out(x, mixed_ab, gate_c, att, gc, w_br_b, w_out_b, row(ln_g, l), row(ln_b, l),
                       layer=l, tm=tm, alpha=alpha)
    return x
```

```python
import functools

import numpy as np
import jax
import jax.numpy as jnp
from jax import lax
from jax.experimental import pallas as pl
from jax.experimental.pallas import tpu as pltpu

F32 = jnp.float32
BF16 = jnp.bfloat16

D_MODEL = 1024
D_BRANCH = 512
GMLP_GROUPS = 4
GMLP_CHUNK = 128
CONV_WIDTH = 31
CONV_HALO = 32
CONV_CHUNK = 128
HEAD_DIM = 64
N_HEADS = 8
MOBA_BLOCK = 256
MOBA_TOPK = 3
N_BRANCH = 3
W_IN_COLS = 10 * D_BRANCH + N_BRANCH * D_MODEL
LN_EPS = 1e-5
MASK_VALUE = -1e30
LANES = 128
SUBLANES = 8
LOG2E = 1.4426950408889634
N_PAIRS = N_HEADS * HEAD_DIM // LANES
MAX_BLOCKS = 32
VMEM_LIMIT_BYTES = 56 * 1024 * 1024

_C_AU, _C_AV, _C_AG = 0, 512, 1024
_C_BVAL, _C_BGLU, _C_BG = 1536, 2048, 2560
_C_Q, _C_K, _C_V, _C_CG = 3072, 3584, 4096, 4608
_C_MERGE = 5120

_PC_ONE, _PC_BLK, _PC_POS = 32, 35, 38
_PC_ROWS = 16
_SUM_ROWS = 16


def _sigmoid(x):
    return 1.0 / (1.0 + jnp.exp(-x))


def _silu(x):
    return x * _sigmoid(x)


def _layer_norm(x, g, b):
    mu = jnp.mean(x, axis=-1, keepdims=True)
    xc = x - mu
    var = jnp.mean(xc * xc, axis=-1, keepdims=True)
    return xc * lax.rsqrt(var + LN_EPS) * g + b


def _dot(a, b):
    return jnp.dot(a, b, preferred_element_type=F32)


def _proj_mix_kernel(x_ref, w_in_ref, sg_w_ref, sg_b_ref, vg_ref, vb_ref, cw_ref, cb_ref,
                     cg_ref, cbeta_ref, wbr_ref,
                     mixed_ref, gate_c_ref, qt_ref, k_ref, vt_ref, gc_ref, kmean_ref,
                     conv_buf, conv_out, shift_buf, mixed_acc, *, tm):
    i = pl.program_id(1)
    xb = x_ref[...].astype(BF16)

    def proj(c0, width):
        return _dot(xb, w_in_ref[:, c0:c0 + width])

    vn = _layer_norm(proj(_C_AV, D_BRANCH), vg_ref[...], vb_ref[...]).astype(BF16)
    row = lax.broadcasted_iota(jnp.int32, (GMLP_CHUNK, GMLP_CHUNK), 0)
    col = lax.broadcasted_iota(jnp.int32, (GMLP_CHUNK, GMLP_CHUNK), 1)
    cgw = D_BRANCH // GMLP_GROUPS
    z_rows = []
    for c in range(tm // GMLP_CHUNK):
        z_groups = []
        for g in range(GMLP_GROUPS):
            wm = jnp.where(col <= row, sg_w_ref[g], 0.0).astype(BF16)
            vcg = vn[c * GMLP_CHUNK:(c + 1) * GMLP_CHUNK, g * cgw:(g + 1) * cgw]
            z_groups.append(_dot(wm, vcg) + sg_b_ref[g])
        z_rows.append(jnp.concatenate(z_groups, axis=1))
    z = jnp.concatenate(z_rows, axis=0)
    y_a = proj(_C_AU, D_BRANCH) * z * _silu(proj(_C_AG, D_BRANCH))
    mixed_acc[...] = _sigmoid(proj(_C_MERGE, D_MODEL)) * _dot(y_a.astype(BF16), wbr_ref[0])

    @pl.when(i == 0)
    def _():
        conv_buf[0:CONV_HALO, :] = jnp.zeros((CONV_HALO, D_BRANCH), F32)

    conv_buf[CONV_HALO:CONV_HALO + tm, :] = proj(_C_BVAL, D_BRANCH) * _sigmoid(proj(_C_BGLU, D_BRANCH))
    rc = CONV_CHUNK
    first = CONV_HALO - (CONV_WIDTH - 1)
    for r in range(tm // rc):
        for lc in range(D_BRANCH // LANES):
            ls = slice(lc * LANES, (lc + 1) * LANES)
            acc = jnp.zeros((rc, LANES), F32)
            for phase in range(SUBLANES):
                taps = [j for j in range(CONV_WIDTH) if (first + j) % SUBLANES == phase]
                if not taps:
                    continue
                offs = [(first + j) // SUBLANES * SUBLANES for j in taps]
                r0 = r * rc + phase
                n = rc + max(offs)
                shift_buf[phase, 0:n, :] = conv_buf[r0:r0 + n, ls]
                for j, a in zip(taps, offs):
                    acc = acc + shift_buf[phase, a:a + rc, :] * cw_ref[j:j + 1, ls]
            conv_out[r * rc:(r + 1) * rc, ls] = acc
    conv = conv_out[...] + cb_ref[...]
    conv_buf[0:CONV_HALO, :] = conv_buf[tm:tm + CONV_HALO, :]
    cv = _silu(_layer_norm(conv, cg_ref[...], cbeta_ref[...]))
    y_b = cv * _silu(proj(_C_BG, D_BRANCH))
    gated_b = _sigmoid(proj(_C_MERGE + D_MODEL, D_MODEL)) * _dot(y_b.astype(BF16), wbr_ref[1])
    mixed_ref[...] = (mixed_acc[...] + gated_b).astype(BF16)

    gate_c_ref[...] = _sigmoid(proj(_C_MERGE + 2 * D_MODEL, D_MODEL)).astype(BF16)
    qt_ref[...] = (proj(_C_Q, D_BRANCH) * (HEAD_DIM ** -0.5 * LOG2E)).T.astype(BF16)
    kf = proj(_C_K, D_BRANCH)
    k_ref[...] = kf.astype(BF16)
    vf = proj(_C_V, D_BRANCH)
    for c in range(tm // MOBA_BLOCK):
        rs = slice(c * MOBA_BLOCK, (c + 1) * MOBA_BLOCK)
        vt_ref[c] = vf[rs, :].T.astype(BF16)
        kmean_ref[c] = jnp.mean(kf[rs, :], axis=0, keepdims=True)
    gc_ref[...] = _silu(proj(_C_CG, D_BRANCH)).astype(BF16)


def _proj_mix(x, w_in, sg_w, sg_b_bc, v_ln_g, v_ln_b, conv_w, conv_b, cv_ln_g, cv_ln_b, w_branch,
              *, layer, tm):
    B, S, D = x.shape
    nb = S // MOBA_BLOCK
    bpt = tm // MOBA_BLOCK
    grid = (B, S // tm)
    row_spec = lambda width: pl.BlockSpec((None, tm, width), lambda b, i: (b, i, 0))
    full = lambda shape: pl.BlockSpec(shape, lambda b, i: (0,) * len(shape))
    vec = full((1, D_BRANCH))
    return pl.pallas_call(
        functools.partial(_proj_mix_kernel, tm=tm),
        grid=grid,
        in_specs=[
            row_spec(D),
            pl.BlockSpec((None, D, W_IN_COLS), lambda b, i: (layer, 0, 0), pipeline_mode=pl.Buffered(1)),
            full((GMLP_GROUPS, GMLP_CHUNK, GMLP_CHUNK)),
            full((GMLP_GROUPS, GMLP_CHUNK, GMLP_CHUNK)),
            vec, vec,
            full((CONV_WIDTH, D_BRANCH)),
            vec, vec, vec,
            pl.BlockSpec((None, 2, D_BRANCH, D_MODEL), lambda b, i: (layer, 0, 0, 0)),
        ],
        out_specs=[row_spec(D_MODEL), row_spec(D_MODEL),
                   pl.BlockSpec((None, D_BRANCH, tm), lambda b, i: (b, 0, i)),
                   row_spec(D_BRANCH),
                   pl.BlockSpec((None, bpt, D_BRANCH, MOBA_BLOCK), lambda b, i: (b, i, 0, 0)),
                   row_spec(D_BRANCH),
                   pl.BlockSpec((None, bpt, 1, D_BRANCH), lambda b, i: (b, i, 0, 0))],
        out_shape=[
            jax.ShapeDtypeStruct((B, S, D_MODEL), BF16),
            jax.ShapeDtypeStruct((B, S, D_MODEL), BF16),
            jax.ShapeDtypeStruct((B, D_BRANCH, S), BF16),
            jax.ShapeDtypeStruct((B, S, D_BRANCH), BF16),
            jax.ShapeDtypeStruct((B, nb, D_BRANCH, MOBA_BLOCK), BF16),
            jax.ShapeDtypeStruct((B, S, D_BRANCH), BF16),
            jax.ShapeDtypeStruct((B, nb, 1, D_BRANCH), F32),
        ],
        scratch_shapes=[pltpu.VMEM((CONV_HALO + tm, D_BRANCH), F32),
                        pltpu.VMEM((tm, D_BRANCH), F32),
                        pltpu.VMEM((SUBLANES, CONV_CHUNK + CONV_HALO, LANES), F32),
                        pltpu.VMEM((tm, D_MODEL), F32)],
        compiler_params=pltpu.CompilerParams(
            dimension_semantics=("arbitrary", "arbitrary"), vmem_limit_bytes=VMEM_LIMIT_BYTES),
        name="proj_mix",
    )(x, w_in, sg_w, sg_b_bc, v_ln_g, v_ln_b, conv_w, conv_b, cv_ln_g, cv_ln_b, w_branch)


def _split_bf16(c, parts=3):
    out, rest = [], np.float32(c)
    for _ in range(parts):
        piece = np.float32(rest.astype(BF16))
        out.append(float(piece))
        rest = np.float32(rest - piece)
    return out


def _moba_kernel(qt_ref, k_ref, vt_ref, pc_ref, km_ref, o_ref, qx_ref, s_ref, m_ref, acc_ref):
    qb = pl.program_id(1)
    blk = MOBA_BLOCK
    nbm = MAX_BLOCKS
    rown = lax.broadcasted_iota(jnp.int32, (nbm, blk), 0)
    row_f = lax.broadcasted_iota(jnp.int32, (LANES, blk), 0)
    head_rows = (row_f < HEAD_DIM, row_f >= HEAD_DIM)
    qbv = jnp.full((nbm, blk), qb, jnp.int32)
    rowp = lax.broadcasted_iota(jnp.int32, (_PC_ROWS, blk), 0) + nbm
    t_pos = (qbv[0:_PC_ROWS, :] * blk + lax.broadcasted_iota(jnp.int32, (_PC_ROWS, blk), 1)).astype(F32)
    neg_inf = jnp.float32(-jnp.inf)
    zeros_tail = jnp.zeros((LANES - nbm - _PC_ROWS, blk), F32)
    ones_rows = jnp.ones((_SUM_ROWS, blk), BF16)

    for p in range(N_PAIRS):
        qtp = qt_ref[p * LANES:(p + 1) * LANES, :]
        km = km_ref[:, p * LANES:(p + 1) * LANES]
        km_hi = km.astype(BF16)
        km_lo = (km - km_hi.astype(F32)).astype(BF16)
        for h in range(2):
            c_parts = _split_bf16(2.0 ** (-(2 * p + h + 1) * 8.0 / N_HEADS) * LOG2E)
            qh = jnp.where(head_rows[h], qtp, jnp.zeros_like(qtp))
            gate = _dot(km_hi, qh) + _dot(km_lo, qh)
            g = jnp.where(rown < qbv, gate, neg_inf)
            selected = rown < 0
            for r in range(MOBA_TOPK):
                mx = jnp.max(g, axis=0, keepdims=True)
                idx = jnp.min(jnp.where(g == mx, rown, nbm), axis=0, keepdims=True)
                pick = rown == idx
                selected = jnp.logical_or(selected, jnp.logical_and(pick, qbv > r))
                g = jnp.where(pick, neg_inf, g)
            mask_rows = jnp.where(selected, 0.0, MASK_VALUE)
            bias_t = -sum(c_parts) * t_pos
            bias_t1 = bias_t.astype(BF16).astype(F32)
            bias_t2 = (bias_t - bias_t1).astype(BF16).astype(F32)
            pos_rows = jnp.zeros((_PC_ROWS, blk), F32)
            for i, piece in enumerate((bias_t1, bias_t2, bias_t - bias_t1 - bias_t2)):
                pos_rows = jnp.where(rowp == _PC_ONE + i, piece, pos_rows)
            for i, c_i in enumerate(c_parts):
                pos_rows = jnp.where(jnp.logical_or(rowp == _PC_BLK + i, rowp == _PC_POS + i), c_i, pos_rows)
            extra = jnp.concatenate([mask_rows, pos_rows, zeros_tail], axis=0)
            qx_ref[2 * p + h] = jnp.concatenate([qh, extra.astype(BF16)], axis=0)

    def values(vt_blk, hd):
        return jnp.concatenate([vt_blk[hd * HEAD_DIM:(hd + 1) * HEAD_DIM, :], ones_rows], axis=0)

    lane_pc = lax.broadcasted_iota(jnp.int32, (blk, LANES), 1)
    n_pairs_slots = (qb + 1) // 2
    n_slots = 2 * n_pairs_slots

    def block_rows(j):
        return pl.ds(j * blk if isinstance(j, int) else pl.multiple_of(j * blk, blk), blk)

    def pos_tile(j, final):
        pc_j = pc_ref[block_rows(j), :]
        if final is False:
            return pc_j
        drop = jnp.logical_and(jnp.full((blk, LANES), final, jnp.int32) != 0, lane_pc < nbm)
        return jnp.where(drop, jnp.zeros_like(pc_j), pc_j)

    def update(hd, s, vt_blk):
        m_old = m_ref[hd]
        m_new = jnp.maximum(m_old, jnp.max(s, axis=0, keepdims=True))
        alpha = jnp.exp2(m_old - m_new)
        pexp = jnp.exp2(s - m_new)
        acc_ref[hd] = alpha * acc_ref[hd] + _dot(values(vt_blk, hd), pexp.astype(BF16))
        m_ref[hd] = m_new

    def phase(src, dst, vt_blk, j_next, final_next):
        lead = 2
        pc_n = pos_tile(j_next, final_next)
        for step in range(N_HEADS + lead):
            if step < N_HEADS:
                kk = jnp.concatenate([k_ref[block_rows(j_next), (step // 2) * LANES:(step // 2 + 1) * LANES], pc_n],
                                     axis=1)
                s_ref[dst, step] = _dot(kk, qx_ref[step])
            if step >= lead:
                update(step - lead, s_ref[src, step - lead], vt_blk)

    m_ref[...] = jnp.full(m_ref.shape, -jnp.inf, F32)
    acc_ref[...] = jnp.zeros(acc_ref.shape, F32)
    pc_0 = pos_tile(0, (qb == 0).astype(jnp.int32))
    for hd in range(N_HEADS):
        kk = jnp.concatenate([k_ref[block_rows(0), (hd // 2) * LANES:(hd // 2 + 1) * LANES], pc_0], axis=1)
        s_ref[0, hd] = _dot(kk, qx_ref[hd])

    def run_slots(t0, count):
        for u in range(count):
            t_next = t0 + u + 1
            final_next = (t_next == n_slots).astype(jnp.int32)
            phase(u % 2, 1 - u % 2, vt_ref[jnp.minimum(t0 + u, qb)], jnp.minimum(t_next, qb), final_next)

    unroll = 4
    n_trips = n_slots // unroll

    def slot_group(i, carry):
        run_slots(unroll * i, unroll)
        return carry

    lax.fori_loop(0, n_trips, slot_group, 0)

    @pl.when(n_slots % unroll != 0)
    def _():
        run_slots(unroll * n_trips, 2)

    key_i = lax.broadcasted_iota(jnp.int32, (blk, blk), 0)
    qry_i = lax.broadcasted_iota(jnp.int32, (blk, blk), 1)
    vt_own = vt_ref[qb]
    for hd in range(N_HEADS):
        update(hd, jnp.where(key_i <= qry_i, s_ref[0, hd], MASK_VALUE), vt_own)

    outs = []
    for hd in range(N_HEADS):
        acc = acc_ref[hd]
        outs.append(acc[0:HEAD_DIM, :] / acc[HEAD_DIM:HEAD_DIM + 1, :])
    o_ref[...] = jnp.concatenate(outs, axis=0).T.astype(o_ref.dtype)


def _moba(qt, k, vt, kmean, pos_const):
    B, S, _ = k.shape
    nb = S // MOBA_BLOCK
    return pl.pallas_call(
        _moba_kernel,
        grid=(B, nb),
        in_specs=[
            pl.BlockSpec((None, D_BRANCH, MOBA_BLOCK), lambda b, i: (b, 0, i)),
            pl.BlockSpec((None, S, D_BRANCH), lambda b, i: (b, 0, 0)),
            pl.BlockSpec((None, nb, D_BRANCH, MOBA_BLOCK), lambda b, i: (b, 0, 0, 0)),
            pl.BlockSpec((S, LANES), lambda b, i: (0, 0)),
            pl.BlockSpec((None, MAX_BLOCKS, D_BRANCH), lambda b, i: (b, 0, 0)),
        ],
        out_specs=pl.BlockSpec((None, MOBA_BLOCK, D_BRANCH), lambda b, i: (b, i, 0)),
        out_shape=jax.ShapeDtypeStruct((B, S, D_BRANCH), BF16),
        scratch_shapes=[
            pltpu.VMEM((N_HEADS, 2 * LANES, MOBA_BLOCK), BF16),
            pltpu.VMEM((2, N_HEADS, MOBA_BLOCK, MOBA_BLOCK), F32),
            pltpu.VMEM((N_HEADS, 1, MOBA_BLOCK), F32),
            pltpu.VMEM((N_HEADS, HEAD_DIM + _SUM_ROWS, MOBA_BLOCK), F32),
        ],
        compiler_params=pltpu.CompilerParams(
            dimension_semantics=("arbitrary", "arbitrary"), vmem_limit_bytes=VMEM_LIMIT_BYTES),
        name="moba",
    )(qt, k, vt, pos_const, kmean)


def _merge_out_kernel(x_ref, mixed_ref, gate_c_ref, att_ref, gc_ref, wbc_ref, wout_ref, g_ref, b_ref,
                      o_ref, *, alpha):
    y_c = (att_ref[...].astype(F32) * gc_ref[...].astype(F32)).astype(BF16)
    mixed = mixed_ref[...].astype(F32) + gate_c_ref[...].astype(F32) * _dot(y_c, wbc_ref[...])
    out = _dot(mixed.astype(BF16), wout_ref[...])
    o_ref[...] = _layer_norm(alpha * x_ref[...] + out, g_ref[...], b_ref[...])


def _merge_out(x, mixed_ab, gate_c, att, gc, w_branch, w_out, ln_g, ln_b, *, layer, tm, alpha):
    B, S, D = x.shape
    row_spec = lambda width: pl.BlockSpec((None, tm, width), lambda b, i: (b, i, 0))
    full = lambda shape: pl.BlockSpec(shape, lambda b, i: (0,) * len(shape))
    return pl.pallas_call(
        functools.partial(_merge_out_kernel, alpha=alpha),
        grid=(B, S // tm),
        in_specs=[row_spec(D), row_spec(D_MODEL), row_spec(D_MODEL), row_spec(D_BRANCH), row_spec(D_BRANCH),
                  pl.BlockSpec((None, None, D_BRANCH, D_MODEL), lambda b, i: (layer, 2, 0, 0)),
                  pl.BlockSpec((None, D_MODEL, D_MODEL), lambda b, i: (layer, 0, 0)),
                  full((1, D_MODEL)), full((1, D_MODEL))],
        out_specs=row_spec(D_MODEL),
        out_shape=jax.ShapeDtypeStruct((B, S, D_MODEL), F32),
        compiler_params=pltpu.CompilerParams(
            dimension_semantics=("arbitrary", "arbitrary"), vmem_limit_bytes=VMEM_LIMIT_BYTES),
        name="merge_out",
    )(x, mixed_ab, gate_c, att, gc, w_branch, w_out, ln_g, ln_b)


def _position_constants(S):
    s = jnp.arange(S, dtype=jnp.int32)
    blk, pos = s // MOBA_BLOCK, s % MOBA_BLOCK
    lane = jnp.arange(LANES, dtype=jnp.int32)[None, :]
    pc = (lane == blk[:, None]).astype(F32)
    pc = jnp.where((lane >= _PC_ONE) & (lane < _PC_ONE + 3), 1.0, pc)
    pc = jnp.where((lane >= _PC_BLK) & (lane < _PC_BLK + 3), (blk * MOBA_BLOCK).astype(F32)[:, None], pc)
    pc = jnp.where((lane >= _PC_POS) & (lane < _PC_POS + 3), pos.astype(F32)[:, None], pc)
    return pc.astype(BF16)


def kernel(x, w_in, sg_w, sg_b, v_ln_g, v_ln_b, conv_w, conv_b, cv_ln_g, cv_ln_b, w_branch, w_out, ln_g, ln_b):
    depth = w_in.shape[0]
    B, S, D = x.shape
    nb = S // MOBA_BLOCK
    assert D == D_MODEL and S % MOBA_BLOCK == 0 and nb <= MAX_BLOCKS == _PC_ONE
    alpha = (2 * depth) ** 0.25
    tm = 512
    pos_const = _position_constants(S)
    w_in_b = w_in.astype(BF16)
    w_br_b = w_branch.astype(BF16)
    w_out_b = w_out.astype(BF16)
    sg_b_bc = jnp.broadcast_to(sg_b[..., None], sg_b.shape + (D_BRANCH // GMLP_GROUPS,))
    row = lambda a, l: a[l][None, :]
    for l in range(depth):
        mixed_ab, gate_c, qt, k, vt, gc, kmean = _proj_mix(
            x, w_in_b, sg_w[l], sg_b_bc[l], row(v_ln_g, l), row(v_ln_b, l), conv_w[l], row(conv_b, l),
            row(cv_ln_g, l), row(cv_ln_b, l), w_br_b, layer=l, tm=tm)
        kmean = jnp.pad(kmean.reshape(B, nb, D_BRANCH), ((0, 0), (0, MAX_BLOCKS - nb), (0, 0)))
        att = _moba(qt, k, vt, kmean, pos_const)
        x = _merge_out(x, mixed_ab, gate_c, att, gc, w_br_b, w_out_b, row(ln_g, l), row(ln_b, l),
                       layer=l, tm=tm, alpha=alpha)
    return x
```

```python
import functools

import numpy as np
import jax
import jax.numpy as jnp
from jax import lax
from jax.experimental import pallas as pl
from jax.experimental.pallas import tpu as pltpu

F32 = jnp.float32
BF16 = jnp.bfloat16

D_MODEL = 1024
D_BRANCH = 512
GMLP_GROUPS = 4
GMLP_CHUNK = 128
CONV_WIDTH = 31
CONV_HALO = 32
CONV_CHUNK = 128
ROW_CHUNK = 64
HEAD_DIM = 64
N_HEADS = 8
MOBA_BLOCK = 256
MOBA_TOPK = 3
N_BRANCH = 3
W_IN_COLS = 10 * D_BRANCH + N_BRANCH * D_MODEL
LN_EPS = 1e-5
MASK_VALUE = -1e30
LANES = 128
SUBLANES = 8
LOG2E = 1.4426950408889634
N_PAIRS = N_HEADS * HEAD_DIM // LANES
MAX_BLOCKS = 32
VMEM_LIMIT_BYTES = 56 * 1024 * 1024

_C_AU, _C_AV, _C_AG = 0, 512, 1024
_C_BVAL, _C_BGLU, _C_BG = 1536, 2048, 2560
_C_Q, _C_K, _C_V, _C_CG = 3072, 3584, 4096, 4608
_C_MERGE = 5120

_PC_ONE, _PC_BLK, _PC_POS = 32, 35, 38
_PC_ROWS = 16
_SUM_ROWS = 16


def _sigmoid(x):
    return 1.0 / (1.0 + jnp.exp(-x))


def _silu(x):
    return x * _sigmoid(x)


def _layer_norm(x, g, b):
    mu = jnp.mean(x, axis=-1, keepdims=True)
    xc = x - mu
    var = jnp.mean(xc * xc, axis=-1, keepdims=True)
    return xc * lax.rsqrt(var + LN_EPS) * g + b


def _dot(a, b):
    return jnp.dot(a, b, preferred_element_type=F32)


def _proj_mix_kernel(x_ref, w_in_ref, sg_w_ref, sg_b_ref, vg_ref, vb_ref, cw_ref, cb_ref,
                     cg_ref, cbeta_ref, wbr_ref,
                     mixed_ref, gate_c_ref, qt_ref, k_ref, vt_ref, gc_ref, kmean_ref,
                     h_ref, xb_ref, conv_buf, shift_buf, vn_ref, y_ref, *, tm):
    i = pl.program_id(1)
    xb_ref[...] = x_ref[...].astype(BF16)
    chunks = [slice(r * ROW_CHUNK, (r + 1) * ROW_CHUNK) for r in range(tm // ROW_CHUNK)]
    cols = lambda c0, width=D_BRANCH: slice(c0, c0 + width)
    c_merge_a, c_merge_b, c_merge_c = _C_MERGE, _C_MERGE + D_MODEL, _C_MERGE + 2 * D_MODEL
    c_z = _C_AV
    c_br_a = _C_AU
    c_conv = _C_BVAL
    c_br_b = _C_AU

    def proj(c0, width=D_BRANCH):
        h_ref[:, cols(c0, width)] = _dot(xb_ref[...], w_in_ref[:, cols(c0, width)])

    proj(_C_AV)
    for rs in chunks:
        vn_ref[rs, :] = _layer_norm(h_ref[rs, cols(_C_AV)], vg_ref[...], vb_ref[...]).astype(BF16)
    proj(_C_AU)
    proj(_C_AG)
    proj(c_merge_a, D_MODEL)
    row = lax.broadcasted_iota(jnp.int32, (GMLP_CHUNK, GMLP_CHUNK), 0)
    col = lax.broadcasted_iota(jnp.int32, (GMLP_CHUNK, GMLP_CHUNK), 1)
    cgw = D_BRANCH // GMLP_GROUPS
    for g in range(GMLP_GROUPS):
        wm = jnp.where(col <= row, sg_w_ref[g], 0.0).astype(BF16)
        for c in range(tm // GMLP_CHUNK):
            rs = slice(c * GMLP_CHUNK, (c + 1) * GMLP_CHUNK)
            h_ref[rs, cols(c_z + g * cgw, cgw)] = _dot(wm, vn_ref[rs, cols(g * cgw, cgw)]) + sg_b_ref[g]
    for rs in chunks:
        y_a = h_ref[rs, cols(_C_AU)] * h_ref[rs, cols(c_z)] * _silu(h_ref[rs, cols(_C_AG)])
        y_ref[rs, :] = y_a.astype(BF16)

    proj(_C_BVAL)
    proj(_C_BGLU)

    @pl.when(i == 0)
    def _():
        conv_buf[0:CONV_HALO, :] = jnp.zeros((CONV_HALO, D_BRANCH), F32)

    for rs in chunks:
        glu = h_ref[rs, cols(_C_BVAL)] * _sigmoid(h_ref[rs, cols(_C_BGLU)])
        conv_buf[CONV_HALO + rs.start:CONV_HALO + rs.stop, :] = glu
    h_ref[:, cols(c_br_a, D_MODEL)] = _dot(y_ref[...], wbr_ref[0])
    rc = CONV_CHUNK
    first = CONV_HALO - (CONV_WIDTH - 1)

    def zero_after(c0):
        bits = pltpu.bitcast(h_ref[0:SUBLANES, cols(c0, LANES)], jnp.uint32)
        return pltpu.bitcast(lax.shift_right_logical(lax.shift_right_logical(bits, jnp.uint32(16)), jnp.uint32(16)),
                             F32)

    def conv_chunk(r, lc, after=None):
        ls = slice(lc * LANES, (lc + 1) * LANES)
        acc = jnp.zeros((rc, LANES), F32)
        if after is not None:
            acc = acc + jnp.concatenate([zero_after(after)] * (rc // SUBLANES), axis=0)
        for phase in range(SUBLANES):
            taps = [j for j in range(CONV_WIDTH) if (first + j) % SUBLANES == phase]
            if not taps:
                continue
            offs = [(first + j) // SUBLANES * SUBLANES for j in taps]
            r0 = r * rc + phase
            n = rc + max(offs)
            shift_buf[phase, 0:n, :] = conv_buf[r0:r0 + n, ls]
            window = shift_buf[phase, 0:n, :]
            for j, a in zip(taps, offs):
                acc = acc + window[a:a + rc, :] * cw_ref[j:j + 1, ls]
        h_ref[r * rc:(r + 1) * rc, cols(c_conv + lc * LANES, LANES)] = acc

    piece = 2 * LANES
    pieces_of = lambda sections: [c0 + off for c0, width in sections for off in range(0, width, piece)]
    fillers = pieces_of(((_C_BG, D_BRANCH), (c_merge_b, D_MODEL), (_C_CG, D_BRANCH)))
    n_regions = tm // rc
    n_lc = D_BRANCH // LANES
    every = (n_regions * n_lc) // len(fillers)
    for r in range(n_regions):
        @pl.when(i >= 0)
        def _(r=r):
            after = None
            for lc in range(n_lc):
                conv_chunk(r, lc, after)
                n = r * n_lc + lc
                if (n + 1) % every == 0:
                    after = fillers[n // every]
                    proj(after, piece)
    conv_buf[0:CONV_HALO, :] = conv_buf[tm:tm + CONV_HALO, :]

    late = pieces_of(((c_merge_c, D_MODEL), (_C_Q, D_BRANCH), (_C_K, D_BRANCH), (_C_V, D_BRANCH)))
    for rs in chunks:
        h_ref[rs, cols(c_merge_a, D_MODEL)] = (
            _sigmoid(h_ref[rs, cols(c_merge_a, D_MODEL)]) * h_ref[rs, cols(c_br_a, D_MODEL)])
        if late:
            proj(late.pop(0), piece)
    for rs in chunks:
        cv = _silu(_layer_norm(h_ref[rs, cols(c_conv)] + cb_ref[...], cg_ref[...], cbeta_ref[...]))
        y_ref[rs, :] = (cv * _silu(h_ref[rs, cols(_C_BG)])).astype(BF16)
        if late:
            proj(late.pop(0), piece)
    for c0 in late:
        proj(c0, piece)

    h_ref[:, cols(c_br_b, D_MODEL)] = _dot(y_ref[...], wbr_ref[1])
    for rs in chunks:
        gc_ref[rs, :] = _silu(h_ref[rs, cols(_C_CG)]).astype(BF16)
        gate_c_ref[rs, :] = _sigmoid(h_ref[rs, cols(c_merge_c, D_MODEL)]).astype(BF16)
        k_ref[rs, :] = h_ref[rs, cols(_C_K)].astype(BF16)
    for c in range(tm // MOBA_BLOCK):
        rs = slice(c * MOBA_BLOCK, (c + 1) * MOBA_BLOCK)
        qt_ref[:, rs] = (h_ref[rs, cols(_C_Q)] * (HEAD_DIM ** -0.5 * LOG2E)).T.astype(BF16)
        vt_ref[c] = h_ref[rs, cols(_C_V)].T.astype(BF16)
        kmean_ref[c] = jnp.mean(h_ref[rs, cols(_C_K)], axis=0, keepdims=True)
    for rs in chunks:
        gated_b = _sigmoid(h_ref[rs, cols(c_merge_b, D_MODEL)]) * h_ref[rs, cols(c_br_b, D_MODEL)]
        mixed_ref[rs, :] = (h_ref[rs, cols(c_merge_a, D_MODEL)] + gated_b).astype(BF16)


def _proj_mix(x, w_in, sg_w, sg_b_bc, v_ln_g, v_ln_b, conv_w, conv_b, cv_ln_g, cv_ln_b, w_branch,
              *, layer, tm):
    B, S, D = x.shape
    nb = S // MOBA_BLOCK
    bpt = tm // MOBA_BLOCK
    grid = (B, S // tm)
    row_spec = lambda width: pl.BlockSpec((None, tm, width), lambda b, i: (b, i, 0))
    full = lambda shape: pl.BlockSpec(shape, lambda b, i: (0,) * len(shape))
    vec = full((1, D_BRANCH))
    return pl.pallas_call(
        functools.partial(_proj_mix_kernel, tm=tm),
        grid=grid,
        in_specs=[
            row_spec(D),
            pl.BlockSpec((None, D, W_IN_COLS), lambda b, i: (layer, 0, 0), pipeline_mode=pl.Buffered(1)),
            full((GMLP_GROUPS, GMLP_CHUNK, GMLP_CHUNK)),
            full((GMLP_GROUPS, GMLP_CHUNK, GMLP_CHUNK)),
            vec, vec,
            full((CONV_WIDTH, D_BRANCH)),
            vec, vec, vec,
            pl.BlockSpec((None, 2, D_BRANCH, D_MODEL), lambda b, i: (layer, 0, 0, 0)),
        ],
        out_specs=[row_spec(D_MODEL), row_spec(D_MODEL),
                   pl.BlockSpec((None, D_BRANCH, tm), lambda b, i: (b, 0, i)),
                   row_spec(D_BRANCH),
                   pl.BlockSpec((None, bpt, D_BRANCH, MOBA_BLOCK), lambda b, i: (b, i, 0, 0)),
                   row_spec(D_BRANCH),
                   pl.BlockSpec((None, bpt, 1, D_BRANCH), lambda b, i: (b, i, 0, 0))],
        out_shape=[
            jax.ShapeDtypeStruct((B, S, D_MODEL), BF16),
            jax.ShapeDtypeStruct((B, S, D_MODEL), BF16),
            jax.ShapeDtypeStruct((B, D_BRANCH, S), BF16),
            jax.ShapeDtypeStruct((B, S, D_BRANCH), BF16),
            jax.ShapeDtypeStruct((B, nb, D_BRANCH, MOBA_BLOCK), BF16),
            jax.ShapeDtypeStruct((B, S, D_BRANCH), BF16),
            jax.ShapeDtypeStruct((B, nb, 1, D_BRANCH), F32),
        ],
        scratch_shapes=[pltpu.VMEM((tm, W_IN_COLS), F32),
                        pltpu.VMEM((tm, D_MODEL), BF16),
                        pltpu.VMEM((CONV_HALO + tm, D_BRANCH), F32),
                        pltpu.VMEM((SUBLANES, CONV_CHUNK + CONV_HALO, LANES), F32),
                        pltpu.VMEM((tm, D_BRANCH), BF16),
                        pltpu.VMEM((tm, D_BRANCH), BF16)],
        compiler_params=pltpu.CompilerParams(
            dimension_semantics=("arbitrary", "arbitrary"), vmem_limit_bytes=VMEM_LIMIT_BYTES),
        name="proj_mix",
    )(x, w_in, sg_w, sg_b_bc, v_ln_g, v_ln_b, conv_w, conv_b, cv_ln_g, cv_ln_b, w_branch)


def _split_bf16(c, parts=3):
    out, rest = [], np.float32(c)
    for _ in range(parts):
        piece = np.float32(rest.astype(BF16))
        out.append(float(piece))
        rest = np.float32(rest - piece)
    return out


def _moba_kernel(qt_ref, k_ref, vt_ref, pc_ref, km_ref, o_ref, qx_ref, s_ref, m_ref, acc_ref):
    qb = pl.program_id(1)
    blk = MOBA_BLOCK
    nbm = MAX_BLOCKS
    rown = lax.broadcasted_iota(jnp.int32, (nbm, blk), 0)
    row_f = lax.broadcasted_iota(jnp.int32, (LANES, blk), 0)
    head_rows = (row_f < HEAD_DIM, row_f >= HEAD_DIM)
    qbv = jnp.full((nbm, blk), qb, jnp.int32)
    rowp = lax.broadcasted_iota(jnp.int32, (_PC_ROWS, blk), 0) + nbm
    t_pos = (qbv[0:_PC_ROWS, :] * blk + lax.broadcasted_iota(jnp.int32, (_PC_ROWS, blk), 1)).astype(F32)
    neg_inf = jnp.float32(-jnp.inf)
    zeros_tail = jnp.zeros((LANES - nbm - _PC_ROWS, blk), F32)
    ones_rows = jnp.ones((_SUM_ROWS, blk), BF16)

    for p in range(N_PAIRS):
        qtp = qt_ref[p * LANES:(p + 1) * LANES, :]
        km = km_ref[:, p * LANES:(p + 1) * LANES]
        km_hi = km.astype(BF16)
        km_lo = (km - km_hi.astype(F32)).astype(BF16)
        for h in range(2):
            c_parts = _split_bf16(2.0 ** (-(2 * p + h + 1) * 8.0 / N_HEADS) * LOG2E)
            qh = jnp.where(head_rows[h], qtp, jnp.zeros_like(qtp))
            gate = _dot(km_hi, qh) + _dot(km_lo, qh)
            g = jnp.where(rown < qbv, gate, neg_inf)
            selected = rown < 0
            for r in range(MOBA_TOPK):
                mx = jnp.max(g, axis=0, keepdims=True)
                idx = jnp.min(jnp.where(g == mx, rown, nbm), axis=0, keepdims=True)
                pick = rown == idx
                selected = jnp.logical_or(selected, jnp.logical_and(pick, qbv > r))
                g = jnp.where(pick, neg_inf, g)
            mask_rows = jnp.where(selected, 0.0, MASK_VALUE)
            bias_t = -sum(c_parts) * t_pos
            bias_t1 = bias_t.astype(BF16).astype(F32)
            bias_t2 = (bias_t - bias_t1).astype(BF16).astype(F32)
            pos_rows = jnp.zeros((_PC_ROWS, blk), F32)
            for i, piece in enumerate((bias_t1, bias_t2, bias_t - bias_t1 - bias_t2)):
                pos_rows = jnp.where(rowp == _PC_ONE + i, piece, pos_rows)
            for i, c_i in enumerate(c_parts):
                pos_rows = jnp.where(jnp.logical_or(rowp == _PC_BLK + i, rowp == _PC_POS + i), c_i, pos_rows)
            extra = jnp.concatenate([mask_rows, pos_rows, zeros_tail], axis=0)
            qx_ref[2 * p + h] = jnp.concatenate([qh, extra.astype(BF16)], axis=0)

    def values(vt_blk, hd):
        return jnp.concatenate([vt_blk[hd * HEAD_DIM:(hd + 1) * HEAD_DIM, :], ones_rows], axis=0)

    lane_pc = lax.broadcasted_iota(jnp.int32, (blk, LANES), 1)
    n_pairs_slots = (qb + 1) // 2
    n_slots = 2 * n_pairs_slots

    def block_rows(j):
        return pl.ds(j * blk if isinstance(j, int) else pl.multiple_of(j * blk, blk), blk)

    def pos_tile(j, final):
        pc_j = pc_ref[block_rows(j), :]
        if final is False:
            return pc_j
        drop = jnp.logical_and(jnp.full((blk, LANES), final, jnp.int32) != 0, lane_pc < nbm)
        return jnp.where(drop, jnp.zeros_like(pc_j), pc_j)

    def update(hd, s, vt_blk):
        m_old = m_ref[hd]
        m_new = jnp.maximum(m_old, jnp.max(s, axis=0, keepdims=True))
        alpha = jnp.exp2(m_old - m_new)
        pexp = jnp.exp2(s - m_new)
        acc_ref[hd] = alpha * acc_ref[hd] + _dot(values(vt_blk, hd), pexp.astype(BF16))
        m_ref[hd] = m_new

    def phase(src, dst, vt_blk, j_next, final_next):
        lead = 2
        pc_n = pos_tile(j_next, final_next)
        for step in range(N_HEADS + lead):
            if step < N_HEADS:
                kk = jnp.concatenate([k_ref[block_rows(j_next), (step // 2) * LANES:(step // 2 + 1) * LANES], pc_n],
                                     axis=1)
                s_ref[dst, step] = _dot(kk, qx_ref[step])
            if step >= lead:
                update(step - lead, s_ref[src, step - lead], vt_blk)

    m_ref[...] = jnp.full(m_ref.shape, -jnp.inf, F32)
    acc_ref[...] = jnp.zeros(acc_ref.shape, F32)
    pc_0 = pos_tile(0, (qb == 0).astype(jnp.int32))
    for hd in range(N_HEADS):
        kk = jnp.concatenate([k_ref[block_rows(0), (hd // 2) * LANES:(hd // 2 + 1) * LANES], pc_0], axis=1)
        s_ref[0, hd] = _dot(kk, qx_ref[hd])

    def run_slots(t0, count):
        for u in range(count):
            t_next = t0 + u + 1
            final_next = (t_next == n_slots).astype(jnp.int32)
            phase(u % 2, 1 - u % 2, vt_ref[jnp.minimum(t0 + u, qb)], jnp.minimum(t_next, qb), final_next)

    unroll = 4
    n_trips = n_slots // unroll

    def slot_group(i, carry):
        run_slots(unroll * i, unroll)
        return carry

    lax.fori_loop(0, n_trips, slot_group, 0)

    @pl.when(n_slots % unroll != 0)
    def _():
        run_slots(unroll * n_trips, 2)

    key_i = lax.broadcasted_iota(jnp.int32, (blk, blk), 0)
    qry_i = lax.broadcasted_iota(jnp.int32, (blk, blk), 1)
    vt_own = vt_ref[qb]
    for hd in range(N_HEADS):
        update(hd, jnp.where(key_i <= qry_i, s_ref[0, hd], MASK_VALUE), vt_own)

    outs = []
    for hd in range(N_HEADS):
        acc = acc_ref[hd]
        outs.append(acc[0:HEAD_DIM, :] / acc[HEAD_DIM:HEAD_DIM + 1, :])
    o_ref[...] = jnp.concatenate(outs, axis=0).T.astype(o_ref.dtype)


def _moba(qt, k, vt, kmean, pos_const):
    B, S, _ = k.shape
    nb = S // MOBA_BLOCK
    return pl.pallas_call(
        _moba_kernel,
        grid=(B, nb),
        in_specs=[
            pl.BlockSpec((None, D_BRANCH, MOBA_BLOCK), lambda b, i: (b, 0, i)),
            pl.BlockSpec((None, S, D_BRANCH), lambda b, i: (b, 0, 0)),
            pl.BlockSpec((None, nb, D_BRANCH, MOBA_BLOCK), lambda b, i: (b, 0, 0, 0)),
            pl.BlockSpec((S, LANES), lambda b, i: (0, 0)),
            pl.BlockSpec((None, MAX_BLOCKS, D_BRANCH), lambda b, i: (b, 0, 0)),
        ],
        out_specs=pl.BlockSpec((None, MOBA_BLOCK, D_BRANCH), lambda b, i: (b, i, 0)),
        out_shape=jax.ShapeDtypeStruct((B, S, D_BRANCH), BF16),
        scratch_shapes=[
            pltpu.VMEM((N_HEADS, 2 * LANES, MOBA_BLOCK), BF16),
            pltpu.VMEM((2, N_HEADS, MOBA_BLOCK, MOBA_BLOCK), F32),
            pltpu.VMEM((N_HEADS, 1, MOBA_BLOCK), F32),
            pltpu.VMEM((N_HEADS, HEAD_DIM + _SUM_ROWS, MOBA_BLOCK), F32),
        ],
        compiler_params=pltpu.CompilerParams(
            dimension_semantics=("arbitrary", "arbitrary"), vmem_limit_bytes=VMEM_LIMIT_BYTES),
        name="moba",
    )(qt, k, vt, pos_const, kmean)


def _merge_out_kernel(x_ref, mixed_ref, gate_c_ref, att_ref, gc_ref, wbc_ref, wout_ref, g_ref, b_ref,
                      o_ref, *, alpha):
    y_c = (att_ref[...].astype(F32) * gc_ref[...].astype(F32)).astype(BF16)
    mixed = mixed_ref[...].astype(F32) + gate_c_ref[...].astype(F32) * _dot(y_c, wbc_ref[...])
    out = _dot(mixed.astype(BF16), wout_ref[...])
    o_ref[...] = _layer_norm(alpha * x_ref[...] + out, g_ref[...], b_ref[...])


def _merge_out(x, mixed_ab, gate_c, att, gc, w_branch, w_out, ln_g, ln_b, *, layer, tm, alpha):
    B, S, D = x.shape
    row_spec = lambda width: pl.BlockSpec((None, tm, width), lambda b, i: (b, i, 0))
    full = lambda shape: pl.BlockSpec(shape, lambda b, i: (0,) * len(shape))
    return pl.pallas_call(
        functools.partial(_merge_out_kernel, alpha=alpha),
        grid=(B, S // tm),
        in_specs=[row_spec(D), row_spec(D_MODEL), row_spec(D_MODEL), row_spec(D_BRANCH), row_spec(D_BRANCH),
                  pl.BlockSpec((None, None, D_BRANCH, D_MODEL), lambda b, i: (layer, 2, 0, 0)),
                  pl.BlockSpec((None, D_MODEL, D_MODEL), lambda b, i: (layer, 0, 0)),
                  full((1, D_MODEL)), full((1, D_MODEL))],
        out_specs=row_spec(D_MODEL),
        out_shape=jax.ShapeDtypeStruct((B, S, D_MODEL), F32),
        compiler_params=pltpu.CompilerParams(
            dimension_semantics=("arbitrary", "arbitrary"), vmem_limit_bytes=VMEM_LIMIT_BYTES),
        name="merge_out",
    )(x, mixed_ab, gate_c, att, gc, w_branch, w_out, ln_g, ln_b)


def _position_constants(S):
    s = jnp.arange(S, dtype=jnp.int32)
    blk, pos = s // MOBA_BLOCK, s % MOBA_BLOCK
    lane = jnp.arange(LANES, dtype=jnp.int32)[None, :]
    pc = (lane == blk[:, None]).astype(F32)
    pc = jnp.where((lane >= _PC_ONE) & (lane < _PC_ONE + 3), 1.0, pc)
    pc = jnp.where((lane >= _PC_BLK) & (lane < _PC_BLK + 3), (blk * MOBA_BLOCK).astype(F32)[:, None], pc)
    pc = jnp.where((lane >= _PC_POS) & (lane < _PC_POS + 3), pos.astype(F32)[:, None], pc)
    return pc.astype(BF16)


def kernel(x, w_in, sg_w, sg_b, v_ln_g, v_ln_b, conv_w, conv_b, cv_ln_g, cv_ln_b, w_branch, w_out, ln_g, ln_b):
    depth = w_in.shape[0]
    B, S, D = x.shape
    nb = S // MOBA_BLOCK
    assert D == D_MODEL and S % MOBA_BLOCK == 0 and nb <= MAX_BLOCKS == _PC_ONE
    alpha = (2 * depth) ** 0.25
    tm = 512
    pos_const = _position_constants(S)
    w_in_b = w_in.astype(BF16)
    w_br_b = w_branch.astype(BF16)
    w_out_b = w_out.astype(BF16)
    sg_b_bc = jnp.broadcast_to(sg_b[..., None], sg_b.shape + (D_BRANCH // GMLP_GROUPS,))
    row = lambda a, l: a[l][None, :]
    for l in range(depth):
        mixed_ab, gate_c, qt, k, vt, gc, kmean = _proj_mix(
            x, w_in_b, sg_w[l], sg_b_bc[l], row(v_ln_g, l), row(v_ln_b, l), conv_w[l], row(conv_b, l),
            row(cv_ln_g, l), row(cv_ln_b, l), w_br_b, layer=l, tm=tm)
        kmean = jnp.pad(kmean.reshape(B, nb, D_BRANCH), ((0, 0), (0, MAX_BLOCKS - nb), (0, 0)))
        att = _moba(qt, k, vt, kmean, pos_const)
        x = _merge_out(x, mixed_ab, gate_c, att, gc, w_br_b, w_out_b, row(ln_g, l), row(ln_b, l),
                       layer=l, tm=tm, alpha=alpha)
    return x
```

```python
import functools

import numpy as np
import jax
import jax.numpy as jnp
from jax import lax
from jax.experimental import pallas as pl
from jax.experimental.pallas import tpu as pltpu

F32 = jnp.float32
BF16 = jnp.bfloat16

D_MODEL = 1024
D_BRANCH = 512
GMLP_GROUPS = 4
GMLP_CHUNK = 128
CONV_WIDTH = 31
CONV_HALO = 32
CONV_CHUNK = 128
ROW_CHUNK = 64
HEAD_DIM = 64
N_HEADS = 8
MOBA_BLOCK = 256
MOBA_TOPK = 3
N_BRANCH = 3
W_IN_COLS = 10 * D_BRANCH + N_BRANCH * D_MODEL
LN_EPS = 1e-5
MASK_VALUE = -1e30
LANES = 128
SUBLANES = 8
LOG2E = 1.4426950408889634
N_PAIRS = N_HEADS * HEAD_DIM // LANES
MAX_BLOCKS = 32
VMEM_LIMIT_BYTES = 56 * 1024 * 1024
FAR_SKIP_HEADS = 4
FAR_SKIP_WINDOW = 8
FAR_SKIP_LOG2 = -100.0

_C_AU, _C_AV, _C_AG = 0, 512, 1024
_C_BVAL, _C_BGLU, _C_BG = 1536, 2048, 2560
_C_Q, _C_K, _C_V, _C_CG = 3072, 3584, 4096, 4608
_C_MERGE = 5120

_PC_ONE, _PC_BLK, _PC_POS = 32, 35, 38
_PC_ROWS = 16
_SUM_ROWS = 16


def _sigmoid(x):
    return 1.0 / (1.0 + jnp.exp(-x))


def _silu(x):
    return x * _sigmoid(x)


def _layer_norm(x, g, b):
    mu = jnp.mean(x, axis=-1, keepdims=True)
    xc = x - mu
    var = jnp.mean(xc * xc, axis=-1, keepdims=True)
    return xc * lax.rsqrt(var + LN_EPS) * g + b


def _dot(a, b):
    return jnp.dot(a, b, preferred_element_type=F32)


def _proj_mix_kernel(x_ref, w_in_ref, sg_w_ref, sg_b_ref, vg_ref, vb_ref, cw_ref, cb_ref,
                     cg_ref, cbeta_ref, wbr_ref, head_ones_ref,
                     mixed_ref, gate_c_ref, qt_ref, k_ref, vt_ref, gc_ref, kmean_ref, knorm_ref,
                     h_ref, xb_ref, conv_buf, shift_buf, vn_ref, y_ref, *, tm):
    i = pl.program_id(1)
    xb_ref[...] = x_ref[...].astype(BF16)
    chunks = [slice(r * ROW_CHUNK, (r + 1) * ROW_CHUNK) for r in range(tm // ROW_CHUNK)]
    cols = lambda c0, width=D_BRANCH: slice(c0, c0 + width)
    c_merge_a, c_merge_b, c_merge_c = _C_MERGE, _C_MERGE + D_MODEL, _C_MERGE + 2 * D_MODEL
    c_z = _C_AV
    c_br_a = _C_AU
    c_conv = _C_BVAL
    c_br_b = _C_AU

    def proj(c0, width=D_BRANCH):
        h_ref[:, cols(c0, width)] = _dot(xb_ref[...], w_in_ref[:, cols(c0, width)])

    proj(_C_AV)
    for rs in chunks:
        vn_ref[rs, :] = _layer_norm(h_ref[rs, cols(_C_AV)], vg_ref[...], vb_ref[...]).astype(BF16)
    proj(_C_AU)
    proj(_C_AG)
    proj(c_merge_a, D_MODEL)
    row = lax.broadcasted_iota(jnp.int32, (GMLP_CHUNK, GMLP_CHUNK), 0)
    col = lax.broadcasted_iota(jnp.int32, (GMLP_CHUNK, GMLP_CHUNK), 1)
    cgw = D_BRANCH // GMLP_GROUPS
    for g in range(GMLP_GROUPS):
        wm = jnp.where(col <= row, sg_w_ref[g], 0.0).astype(BF16)
        for c in range(tm // GMLP_CHUNK):
            rs = slice(c * GMLP_CHUNK, (c + 1) * GMLP_CHUNK)
            h_ref[rs, cols(c_z + g * cgw, cgw)] = _dot(wm, vn_ref[rs, cols(g * cgw, cgw)]) + sg_b_ref[g]
    for rs in chunks:
        y_a = h_ref[rs, cols(_C_AU)] * h_ref[rs, cols(c_z)] * _silu(h_ref[rs, cols(_C_AG)])
        y_ref[rs, :] = y_a.astype(BF16)

    proj(_C_BVAL)
    proj(_C_BGLU)

    @pl.when(i == 0)
    def _():
        conv_buf[0:CONV_HALO, :] = jnp.zeros((CONV_HALO, D_BRANCH), F32)

    for rs in chunks:
        glu = h_ref[rs, cols(_C_BVAL)] * _sigmoid(h_ref[rs, cols(_C_BGLU)])
        conv_buf[CONV_HALO + rs.start:CONV_HALO + rs.stop, :] = glu
    h_ref[:, cols(c_br_a, D_MODEL)] = _dot(y_ref[...], wbr_ref[0])
    rc = CONV_CHUNK
    first = CONV_HALO - (CONV_WIDTH - 1)

    def zero_after(c0):
        bits = pltpu.bitcast(h_ref[0:SUBLANES, cols(c0, LANES)], jnp.uint32)
        return pltpu.bitcast(lax.shift_right_logical(lax.shift_right_logical(bits, jnp.uint32(16)), jnp.uint32(16)),
                             F32)

    def conv_chunk(r, lc, after=None):
        ls = slice(lc * LANES, (lc + 1) * LANES)
        acc = jnp.zeros((rc, LANES), F32)
        if after is not None:
            acc = acc + jnp.concatenate([zero_after(after)] * (rc // SUBLANES), axis=0)
        for phase in range(SUBLANES):
            taps = [j for j in range(CONV_WIDTH) if (first + j) % SUBLANES == phase]
            if not taps:
                continue
            offs = [(first + j) // SUBLANES * SUBLANES for j in taps]
            r0 = r * rc + phase
            n = rc + max(offs)
            shift_buf[phase, 0:n, :] = conv_buf[r0:r0 + n, ls]
            window = shift_buf[phase, 0:n, :]
            for j, a in zip(taps, offs):
                acc = acc + window[a:a + rc, :] * cw_ref[j:j + 1, ls]
        h_ref[r * rc:(r + 1) * rc, cols(c_conv + lc * LANES, LANES)] = acc

    piece = 2 * LANES
    pieces_of = lambda sections: [c0 + off for c0, width in sections for off in range(0, width, piece)]
    fillers = pieces_of(((_C_BG, D_BRANCH), (c_merge_b, D_MODEL), (_C_CG, D_BRANCH)))
    n_regions = tm // rc
    n_lc = D_BRANCH // LANES
    every = (n_regions * n_lc) // len(fillers)
    for r in range(n_regions):
        @pl.when(i >= 0)
        def _(r=r):
            after = None
            for lc in range(n_lc):
                conv_chunk(r, lc, after)
                n = r * n_lc + lc
                if (n + 1) % every == 0:
                    after = fillers[n // every]
                    proj(after, piece)
    conv_buf[0:CONV_HALO, :] = conv_buf[tm:tm + CONV_HALO, :]

    late = pieces_of(((c_merge_c, D_MODEL), (_C_Q, D_BRANCH), (_C_K, D_BRANCH), (_C_V, D_BRANCH)))
    for rs in chunks:
        h_ref[rs, cols(c_merge_a, D_MODEL)] = (
            _sigmoid(h_ref[rs, cols(c_merge_a, D_MODEL)]) * h_ref[rs, cols(c_br_a, D_MODEL)])
        if late:
            proj(late.pop(0), piece)
    for rs in chunks:
        cv = _silu(_layer_norm(h_ref[rs, cols(c_conv)] + cb_ref[...], cg_ref[...], cbeta_ref[...]))
        y_ref[rs, :] = (cv * _silu(h_ref[rs, cols(_C_BG)])).astype(BF16)
        if late:
            proj(late.pop(0), piece)
    for c0 in late:
        proj(c0, piece)

    h_ref[:, cols(c_br_b, D_MODEL)] = _dot(y_ref[...], wbr_ref[1])
    for rs in chunks:
        gc_ref[rs, :] = _silu(h_ref[rs, cols(_C_CG)]).astype(BF16)
        gate_c_ref[rs, :] = _sigmoid(h_ref[rs, cols(c_merge_c, D_MODEL)]).astype(BF16)
        k_ref[rs, :] = h_ref[rs, cols(_C_K)].astype(BF16)
    for c in range(tm // MOBA_BLOCK):
        rs = slice(c * MOBA_BLOCK, (c + 1) * MOBA_BLOCK)
        qt_ref[:, rs] = (h_ref[rs, cols(_C_Q)] * (HEAD_DIM ** -0.5 * LOG2E)).T.astype(BF16)
        vt_ref[c] = h_ref[rs, cols(_C_V)].T.astype(BF16)
        kblk = h_ref[rs, cols(_C_K)]
        kmean_ref[c] = jnp.mean(kblk, axis=0, keepdims=True)
        norm2 = _dot((kblk * kblk).astype(BF16), head_ones_ref[...])
        knorm_ref[c] = jnp.max(norm2, axis=0, keepdims=True)
    for rs in chunks:
        gated_b = _sigmoid(h_ref[rs, cols(c_merge_b, D_MODEL)]) * h_ref[rs, cols(c_br_b, D_MODEL)]
        mixed_ref[rs, :] = (h_ref[rs, cols(c_merge_a, D_MODEL)] + gated_b).astype(BF16)


def _proj_mix(x, w_in, sg_w, sg_b_bc, v_ln_g, v_ln_b, conv_w, conv_b, cv_ln_g, cv_ln_b, w_branch, head_ones,
              *, layer, tm):
    B, S, D = x.shape
    nb = S // MOBA_BLOCK
    bpt = tm // MOBA_BLOCK
    grid = (B, S // tm)
    row_spec = lambda width: pl.BlockSpec((None, tm, width), lambda b, i: (b, i, 0))
    full = lambda shape: pl.BlockSpec(shape, lambda b, i: (0,) * len(shape))
    vec = full((1, D_BRANCH))
    return pl.pallas_call(
        functools.partial(_proj_mix_kernel, tm=tm),
        grid=grid,
        in_specs=[
            row_spec(D),
            pl.BlockSpec((None, D, W_IN_COLS), lambda b, i: (layer, 0, 0), pipeline_mode=pl.Buffered(1)),
            full((GMLP_GROUPS, GMLP_CHUNK, GMLP_CHUNK)),
            full((GMLP_GROUPS, GMLP_CHUNK, GMLP_CHUNK)),
            vec, vec,
            full((CONV_WIDTH, D_BRANCH)),
            vec, vec, vec,
            pl.BlockSpec((None, 2, D_BRANCH, D_MODEL), lambda b, i: (layer, 0, 0, 0)),
            full((D_BRANCH, LANES)),
        ],
        out_specs=[row_spec(D_MODEL), row_spec(D_MODEL),
                   pl.BlockSpec((None, D_BRANCH, tm), lambda b, i: (b, 0, i)),
                   row_spec(D_BRANCH),
                   pl.BlockSpec((None, bpt, D_BRANCH, MOBA_BLOCK), lambda b, i: (b, i, 0, 0)),
                   row_spec(D_BRANCH),
                   pl.BlockSpec((None, bpt, 1, D_BRANCH), lambda b, i: (b, i, 0, 0)),
                   pl.BlockSpec((None, bpt, 1, LANES), lambda b, i: (b, i, 0, 0))],
        out_shape=[
            jax.ShapeDtypeStruct((B, S, D_MODEL), BF16),
            jax.ShapeDtypeStruct((B, S, D_MODEL), BF16),
            jax.ShapeDtypeStruct((B, D_BRANCH, S), BF16),
            jax.ShapeDtypeStruct((B, S, D_BRANCH), BF16),
            jax.ShapeDtypeStruct((B, nb, D_BRANCH, MOBA_BLOCK), BF16),
            jax.ShapeDtypeStruct((B, S, D_BRANCH), BF16),
            jax.ShapeDtypeStruct((B, nb, 1, D_BRANCH), F32),
            jax.ShapeDtypeStruct((B, nb, 1, LANES), F32),
        ],
        scratch_shapes=[pltpu.VMEM((tm, W_IN_COLS), F32),
                        pltpu.VMEM((tm, D_MODEL), BF16),
                        pltpu.VMEM((CONV_HALO + tm, D_BRANCH), F32),
                        pltpu.VMEM((SUBLANES, CONV_CHUNK + CONV_HALO, LANES), F32),
                        pltpu.VMEM((tm, D_BRANCH), BF16),
                        pltpu.VMEM((tm, D_BRANCH), BF16)],
        compiler_params=pltpu.CompilerParams(
            dimension_semantics=("arbitrary", "arbitrary"), vmem_limit_bytes=VMEM_LIMIT_BYTES),
        name="proj_mix",
    )(x, w_in, sg_w, sg_b_bc, v_ln_g, v_ln_b, conv_w, conv_b, cv_ln_g, cv_ln_b, w_branch, head_ones)


def _split_bf16(c, parts=3):
    out, rest = [], np.float32(c)
    for _ in range(parts):
        piece = np.float32(rest.astype(BF16))
        out.append(float(piece))
        rest = np.float32(rest - piece)
    return out


def _moba_kernel(qt_ref, k_ref, vt_ref, pc_ref, km_ref, kn_ref, o_ref, qx_ref, s_ref, m_ref, acc_ref):
    qb = pl.program_id(1)
    blk = MOBA_BLOCK
    nbm = MAX_BLOCKS
    rown = lax.broadcasted_iota(jnp.int32, (nbm, blk), 0)
    row_f = lax.broadcasted_iota(jnp.int32, (LANES, blk), 0)
    head_rows = (row_f < HEAD_DIM, row_f >= HEAD_DIM)
    qbv = jnp.full((nbm, blk), qb, jnp.int32)
    rowp = lax.broadcasted_iota(jnp.int32, (_PC_ROWS, blk), 0) + nbm
    t_pos = (qbv[0:_PC_ROWS, :] * blk + lax.broadcasted_iota(jnp.int32, (_PC_ROWS, blk), 1)).astype(F32)
    neg_inf = jnp.float32(-jnp.inf)
    zeros_tail = jnp.zeros((LANES - nbm - _PC_ROWS, blk), F32)
    ones_rows = jnp.ones((_SUM_ROWS, blk), BF16)

    for p in range(N_PAIRS):
        qtp = qt_ref[p * LANES:(p + 1) * LANES, :]
        km = km_ref[:, p * LANES:(p + 1) * LANES]
        km_hi = km.astype(BF16)
        km_lo = (km - km_hi.astype(F32)).astype(BF16)
        for h in range(2):
            c_parts = _split_bf16(2.0 ** (-(2 * p + h + 1) * 8.0 / N_HEADS) * LOG2E)
            qh = jnp.where(head_rows[h], qtp, jnp.zeros_like(qtp))
            gate = _dot(km_hi, qh) + _dot(km_lo, qh)
            g = jnp.where(rown < qbv, gate, neg_inf)
            selected = rown < 0
            for r in range(MOBA_TOPK):
                mx = jnp.max(g, axis=0, keepdims=True)
                idx = jnp.min(jnp.where(g == mx, rown, nbm), axis=0, keepdims=True)
                pick = rown == idx
                selected = jnp.logical_or(selected, jnp.logical_and(pick, qbv > r))
                g = jnp.where(pick, neg_inf, g)
            mask_rows = jnp.where(selected, 0.0, MASK_VALUE)
            bias_t = -sum(c_parts) * t_pos
            bias_t1 = bias_t.astype(BF16).astype(F32)
            bias_t2 = (bias_t - bias_t1).astype(BF16).astype(F32)
            pos_rows = jnp.zeros((_PC_ROWS, blk), F32)
            for i, piece in enumerate((bias_t1, bias_t2, bias_t - bias_t1 - bias_t2)):
                pos_rows = jnp.where(rowp == _PC_ONE + i, piece, pos_rows)
            for i, c_i in enumerate(c_parts):
                pos_rows = jnp.where(jnp.logical_or(rowp == _PC_BLK + i, rowp == _PC_POS + i), c_i, pos_rows)
            extra = jnp.concatenate([mask_rows, pos_rows, zeros_tail], axis=0)
            qx_ref[2 * p + h] = jnp.concatenate([qh, extra.astype(BF16)], axis=0)

    def values(vt_blk, hd):
        return jnp.concatenate([vt_blk[hd * HEAD_DIM:(hd + 1) * HEAD_DIM, :], ones_rows], axis=0)

    lane_pc = lax.broadcasted_iota(jnp.int32, (blk, LANES), 1)
    n_pairs_slots = (qb + 1) // 2
    n_slots = 2 * n_pairs_slots

    def block_rows(j):
        return pl.ds(j * blk if isinstance(j, int) else pl.multiple_of(j * blk, blk), blk)

    def pos_tile(j, final):
        pc_j = pc_ref[block_rows(j), :]
        if final is False:
            return pc_j
        drop = jnp.logical_and(jnp.full((blk, LANES), final, jnp.int32) != 0, lane_pc < nbm)
        return jnp.where(drop, jnp.zeros_like(pc_j), pc_j)

    def update(hd, s, vt_blk):
        m_old = m_ref[hd]
        m_new = jnp.maximum(m_old, jnp.max(s, axis=0, keepdims=True))
        alpha = jnp.exp2(m_old - m_new)
        pexp = jnp.exp2(s - m_new)
        acc_ref[hd] = alpha * acc_ref[hd] + _dot(values(vt_blk, hd), pexp.astype(BF16))
        m_ref[hd] = m_new

    def scores(dst, hd, j, pc_j):
        kk = jnp.concatenate([k_ref[block_rows(j), (hd // 2) * LANES:(hd // 2 + 1) * LANES], pc_j], axis=1)
        s_ref[dst, hd] = _dot(kk, qx_ref[hd])

    def phase(src, dst, vt_blk, j_next, final_next, heads):
        lead = 2
        pc_n = pos_tile(j_next, final_next)
        for step in range(len(heads) + lead):
            if step < len(heads):
                scores(dst, heads[step], j_next, pc_n)
            if step >= lead:
                update(heads[step - lead], s_ref[src, heads[step - lead]], vt_blk)

    def run_slots(t0, count, heads):
        for u in range(count):
            t_next = t0 + u + 1
            final_next = (t_next == n_slots).astype(jnp.int32)
            phase(u % 2, 1 - u % 2, vt_ref[jnp.minimum(t0 + u, qb)], jnp.minimum(t_next, qb), final_next, heads)

    unroll = 4
    steep, shallow = tuple(range(FAR_SKIP_HEADS)), tuple(range(FAR_SKIP_HEADS, N_HEADS))
    t_far_max = jnp.maximum(qb - FAR_SKIP_WINDOW, 0) // unroll * unroll
    key_dist = ((qbv - rown) * blk + lax.broadcasted_iota(jnp.int32, (nbm, blk), 1) - (blk - 1)).astype(F32)
    skip_ok = None
    for p in range(FAR_SKIP_HEADS // 2):
        kt_own = k_ref[block_rows(qb), p * LANES:(p + 1) * LANES].astype(F32).T
        for h in range(2):
            hd = 2 * p + h
            c_hd = 2.0 ** (-(hd + 1) * 8.0 / N_HEADS) * LOG2E
            rows = slice(hd * HEAD_DIM, (hd + 1) * HEAD_DIM)
            qf = qt_ref[rows, :].astype(F32)
            q_norm = jnp.sqrt(jnp.sum(qf * qf, axis=0, keepdims=True))
            diag = jnp.sum(qf * kt_own[h * HEAD_DIM:(h + 1) * HEAD_DIM, :], axis=0, keepdims=True)
            k_norm = jnp.sqrt(kn_ref[:, hd:hd + 1])
            slack = 0.02 * q_norm * jnp.max(k_norm, axis=0, keepdims=True) + 1.0
            bound = 1.02 * k_norm * q_norm - c_hd * key_dist - diag + slack
            masked = jnp.where(rown < t_far_max, bound, neg_inf)
            worst = jnp.max(jnp.max(masked, axis=0, keepdims=True), axis=1, keepdims=True)[0, 0]
            ok = worst < FAR_SKIP_LOG2
            skip_ok = ok if skip_ok is None else jnp.logical_and(skip_ok, ok)
    t_far = jnp.where(skip_ok, t_far_max, 0)

    m_ref[...] = jnp.full(m_ref.shape, -jnp.inf, F32)
    acc_ref[...] = jnp.zeros(acc_ref.shape, F32)
    first_final = (qb == 0).astype(jnp.int32)
    pc_0 = pos_tile(0, first_final)
    for hd in shallow:
        scores(0, hd, 0, pc_0)

    def far_group(i, carry):
        run_slots(unroll * i, unroll, shallow)
        return carry

    lax.fori_loop(0, t_far // unroll, far_group, 0)

    pc_far = pos_tile(jnp.minimum(t_far, qb), jnp.where(t_far == 0, first_final, 0))
    for hd in steep:
        scores(0, hd, jnp.minimum(t_far, qb), pc_far)

    all_heads = steep + shallow
    n_trips = (n_slots - t_far) // unroll

    def slot_group(i, carry):
        run_slots(t_far + unroll * i, unroll, all_heads)
        return carry

    lax.fori_loop(0, n_trips, slot_group, 0)

    @pl.when((n_slots - t_far) % unroll != 0)
    def _():
        run_slots(t_far + unroll * n_trips, 2, all_heads)

    key_i = lax.broadcasted_iota(jnp.int32, (blk, blk), 0)
    qry_i = lax.broadcasted_iota(jnp.int32, (blk, blk), 1)
    vt_own = vt_ref[qb]
    for hd in range(N_HEADS):
        update(hd, jnp.where(key_i <= qry_i, s_ref[0, hd], MASK_VALUE), vt_own)

    outs = []
    for hd in range(N_HEADS):
        acc = acc_ref[hd]
        outs.append(acc[0:HEAD_DIM, :] / acc[HEAD_DIM:HEAD_DIM + 1, :])
    o_ref[...] = jnp.concatenate(outs, axis=0).T.astype(o_ref.dtype)


def _moba(qt, k, vt, kmean, knorm, pos_const):
    B, S, _ = k.shape
    nb = S // MOBA_BLOCK
    return pl.pallas_call(
        _moba_kernel,
        grid=(B, nb),
        in_specs=[
            pl.BlockSpec((None, D_BRANCH, MOBA_BLOCK), lambda b, i: (b, 0, i)),
            pl.BlockSpec((None, S, D_BRANCH), lambda b, i: (b, 0, 0)),
            pl.BlockSpec((None, nb, D_BRANCH, MOBA_BLOCK), lambda b, i: (b, 0, 0, 0)),
            pl.BlockSpec((S, LANES), lambda b, i: (0, 0)),
            pl.BlockSpec((None, MAX_BLOCKS, D_BRANCH), lambda b, i: (b, 0, 0)),
            pl.BlockSpec((None, MAX_BLOCKS, LANES), lambda b, i: (b, 0, 0)),
        ],
        out_specs=pl.BlockSpec((None, MOBA_BLOCK, D_BRANCH), lambda b, i: (b, i, 0)),
        out_shape=jax.ShapeDtypeStruct((B, S, D_BRANCH), BF16),
        scratch_shapes=[
            pltpu.VMEM((N_HEADS, 2 * LANES, MOBA_BLOCK), BF16),
            pltpu.VMEM((2, N_HEADS, MOBA_BLOCK, MOBA_BLOCK), F32),
            pltpu.VMEM((N_HEADS, 1, MOBA_BLOCK), F32),
            pltpu.VMEM((N_HEADS, HEAD_DIM + _SUM_ROWS, MOBA_BLOCK), F32),
        ],
        compiler_params=pltpu.CompilerParams(
            dimension_semantics=("arbitrary", "arbitrary"), vmem_limit_bytes=VMEM_LIMIT_BYTES),
        name="moba",
    )(qt, k, vt, pos_const, kmean, knorm)


def _merge_out_kernel(x_ref, mixed_ref, gate_c_ref, att_ref, gc_ref, wbc_ref, wout_ref, g_ref, b_ref,
                      o_ref, *, alpha):
    y_c = (att_ref[...].astype(F32) * gc_ref[...].astype(F32)).astype(BF16)
    mixed = mixed_ref[...].astype(F32) + gate_c_ref[...].astype(F32) * _dot(y_c, wbc_ref[...])
    out = _dot(mixed.astype(BF16), wout_ref[...])
    o_ref[...] = _layer_norm(alpha * x_ref[...] + out, g_ref[...], b_ref[...])


def _merge_out(x, mixed_ab, gate_c, att, gc, w_branch, w_out, ln_g, ln_b, *, layer, tm, alpha):
    B, S, D = x.shape
    row_spec = lambda width: pl.BlockSpec((None, tm, width), lambda b, i: (b, i, 0))
    full = lambda shape: pl.BlockSpec(shape, lambda b, i: (0,) * len(shape))
    return pl.pallas_call(
        functools.partial(_merge_out_kernel, alpha=alpha),
        grid=(B, S // tm),
        in_specs=[row_spec(D), row_spec(D_MODEL), row_spec(D_MODEL), row_spec(D_BRANCH), row_spec(D_BRANCH),
                  pl.BlockSpec((None, None, D_BRANCH, D_MODEL), lambda b, i: (layer, 2, 0, 0)),
                  pl.BlockSpec((None, D_MODEL, D_MODEL), lambda b, i: (layer, 0, 0)),
                  full((1, D_MODEL)), full((1, D_MODEL))],
        out_specs=row_spec(D_MODEL),
        out_shape=jax.ShapeDtypeStruct((B, S, D_MODEL), F32),
        compiler_params=pltpu.CompilerParams(
            dimension_semantics=("arbitrary", "arbitrary"), vmem_limit_bytes=VMEM_LIMIT_BYTES),
        name="merge_out",
    )(x, mixed_ab, gate_c, att, gc, w_branch, w_out, ln_g, ln_b)


def _position_constants(S):
    s = jnp.arange(S, dtype=jnp.int32)
    blk, pos = s // MOBA_BLOCK, s % MOBA_BLOCK
    lane = jnp.arange(LANES, dtype=jnp.int32)[None, :]
    pc = (lane == blk[:, None]).astype(F32)
    pc = jnp.where((lane >= _PC_ONE) & (lane < _PC_ONE + 3), 1.0, pc)
    pc = jnp.where((lane >= _PC_BLK) & (lane < _PC_BLK + 3), (blk * MOBA_BLOCK).astype(F32)[:, None], pc)
    pc = jnp.where((lane >= _PC_POS) & (lane < _PC_POS + 3), pos.astype(F32)[:, None], pc)
    return pc.astype(BF16)


def kernel(x, w_in, sg_w, sg_b, v_ln_g, v_ln_b, conv_w, conv_b, cv_ln_g, cv_ln_b, w_branch, w_out, ln_g, ln_b):
    depth = w_in.shape[0]
    B, S, D = x.shape
    nb = S // MOBA_BLOCK
    assert D == D_MODEL and S % MOBA_BLOCK == 0 and nb <= MAX_BLOCKS == _PC_ONE
    alpha = (2 * depth) ** 0.25
    tm = 512
    pos_const = _position_constants(S)
    w_in_b = w_in.astype(BF16)
    w_br_b = w_branch.astype(BF16)
    w_out_b = w_out.astype(BF16)
    head_ones = (jnp.arange(D_BRANCH)[:, None] // HEAD_DIM == jnp.arange(LANES)[None, :]).astype(BF16)
    sg_b_bc = jnp.broadcast_to(sg_b[..., None], sg_b.shape + (D_BRANCH // GMLP_GROUPS,))
    row = lambda a, l: a[l][None, :]
    for l in range(depth):
        mixed_ab, gate_c, qt, k, vt, gc, kmean, knorm = _proj_mix(
            x, w_in_b, sg_w[l], sg_b_bc[l], row(v_ln_g, l), row(v_ln_b, l), conv_w[l], row(conv_b, l),
            row(cv_ln_g, l), row(cv_ln_b, l), w_br_b, head_ones, layer=l, tm=tm)
        pad_blocks = lambda a: jnp.pad(a.reshape(B, nb, -1), ((0, 0), (0, MAX_BLOCKS - nb), (0, 0)))
        att = _moba(qt, k, vt, pad_blocks(kmean), pad_blocks(knorm), pos_const)
        x = _merge_out(x, mixed_ab, gate_c, att, gc, w_br_b, w_out_b, row(ln_g, l), row(ln_b, l),
                       layer=l, tm=tm, alpha=alpha)
    return x
```

```python
import functools

import numpy as np
import jax
import jax.numpy as jnp
from jax import lax
from jax.experimental import pallas as pl
from jax.experimental.pallas import tpu as pltpu

F32 = jnp.float32
BF16 = jnp.bfloat16

D_MODEL = 1024
D_BRANCH = 512
GMLP_GROUPS = 4
GMLP_CHUNK = 128
CONV_WIDTH = 31
CONV_HALO = 32
CONV_CHUNK = 128
ROW_CHUNK = 64
HEAD_DIM = 64
N_HEADS = 8
MOBA_BLOCK = 256
MOBA_TOPK = 3
N_BRANCH = 3
W_IN_COLS = 10 * D_BRANCH + N_BRANCH * D_MODEL
LN_EPS = 1e-5
MASK_VALUE = -1e30
LANES = 128
SUBLANES = 8
LOG2E = 1.4426950408889634
N_PAIRS = N_HEADS * HEAD_DIM // LANES
MAX_BLOCKS = 32
VMEM_LIMIT_BYTES = 56 * 1024 * 1024
FAR_SKIP_CLASSES = (((4,), 12), ((2, 3), 6), ((0, 1), 2))
FAR_SKIP_LOG2 = -100.0

_C_AU, _C_AV, _C_AG = 0, 512, 1024
_C_BVAL, _C_BGLU, _C_BG = 1536, 2048, 2560
_C_Q, _C_K, _C_V, _C_CG = 3072, 3584, 4096, 4608
_C_MERGE = 5120

_PC_ONE, _PC_BLK, _PC_POS = 32, 35, 38
_PC_ROWS = 16
_SUM_ROWS = 16


def _sigmoid(x):
    return 1.0 / (1.0 + jnp.exp(-x))


def _silu(x):
    return x * _sigmoid(x)


def _layer_norm(x, g, b):
    mu = jnp.mean(x, axis=-1, keepdims=True)
    xc = x - mu
    var = jnp.mean(xc * xc, axis=-1, keepdims=True)
    return xc * lax.rsqrt(var + LN_EPS) * g + b


def _dot(a, b):
    return jnp.dot(a, b, preferred_element_type=F32)


def _proj_mix_kernel(x_ref, w_in_ref, sg_w_ref, sg_b_ref, vg_ref, vb_ref, cw_ref, cb_ref,
                     cg_ref, cbeta_ref, wbr_ref, head_ones_ref,
                     mixed_ref, gate_c_ref, qt_ref, k_ref, vt_ref, gc_ref, kmean_ref, knorm_ref,
                     h_ref, xb_ref, conv_buf, shift_buf, vn_ref, y_ref, *, tm):
    i = pl.program_id(1)
    xb_ref[...] = x_ref[...].astype(BF16)
    chunks = [slice(r * ROW_CHUNK, (r + 1) * ROW_CHUNK) for r in range(tm // ROW_CHUNK)]
    cols = lambda c0, width=D_BRANCH: slice(c0, c0 + width)
    c_merge_a, c_merge_b, c_merge_c = _C_MERGE, _C_MERGE + D_MODEL, _C_MERGE + 2 * D_MODEL
    c_z = _C_AV
    c_br_a = _C_AU
    c_conv = _C_BVAL
    c_br_b = _C_AU

    def proj(c0, width=D_BRANCH):
        h_ref[:, cols(c0, width)] = _dot(xb_ref[...], w_in_ref[:, cols(c0, width)])

    proj(_C_AV)
    for rs in chunks:
        vn_ref[rs, :] = _layer_norm(h_ref[rs, cols(_C_AV)], vg_ref[...], vb_ref[...]).astype(BF16)
    proj(_C_AU)
    proj(_C_AG)
    proj(c_merge_a, D_MODEL)
    row = lax.broadcasted_iota(jnp.int32, (GMLP_CHUNK, GMLP_CHUNK), 0)
    col = lax.broadcasted_iota(jnp.int32, (GMLP_CHUNK, GMLP_CHUNK), 1)
    cgw = D_BRANCH // GMLP_GROUPS
    for g in range(GMLP_GROUPS):
        wm = jnp.where(col <= row, sg_w_ref[g], 0.0).astype(BF16)
        for c in range(tm // GMLP_CHUNK):
            rs = slice(c * GMLP_CHUNK, (c + 1) * GMLP_CHUNK)
            h_ref[rs, cols(c_z + g * cgw, cgw)] = _dot(wm, vn_ref[rs, cols(g * cgw, cgw)]) + sg_b_ref[g]
    for rs in chunks:
        y_a = h_ref[rs, cols(_C_AU)] * h_ref[rs, cols(c_z)] * _silu(h_ref[rs, cols(_C_AG)])
        y_ref[rs, :] = y_a.astype(BF16)

    proj(_C_BVAL)
    proj(_C_BGLU)

    @pl.when(i == 0)
    def _():
        conv_buf[0:CONV_HALO, :] = jnp.zeros((CONV_HALO, D_BRANCH), F32)

    for rs in chunks:
        glu = h_ref[rs, cols(_C_BVAL)] * _sigmoid(h_ref[rs, cols(_C_BGLU)])
        conv_buf[CONV_HALO + rs.start:CONV_HALO + rs.stop, :] = glu
    h_ref[:, cols(c_br_a, D_MODEL)] = _dot(y_ref[...], wbr_ref[0])
    rc = CONV_CHUNK
    first = CONV_HALO - (CONV_WIDTH - 1)

    def zero_after(c0):
        bits = pltpu.bitcast(h_ref[0:SUBLANES, cols(c0, LANES)], jnp.uint32)
        return pltpu.bitcast(lax.shift_right_logical(lax.shift_right_logical(bits, jnp.uint32(16)), jnp.uint32(16)),
                             F32)

    def conv_chunk(r, lc, after=None):
        ls = slice(lc * LANES, (lc + 1) * LANES)
        acc = jnp.zeros((rc, LANES), F32)
        if after is not None:
            acc = acc + jnp.concatenate([zero_after(after)] * (rc // SUBLANES), axis=0)
        for phase in range(SUBLANES):
            taps = [j for j in range(CONV_WIDTH) if (first + j) % SUBLANES == phase]
            if not taps:
                continue
            offs = [(first + j) // SUBLANES * SUBLANES for j in taps]
            r0 = r * rc + phase
            n = rc + max(offs)
            shift_buf[phase, 0:n, :] = conv_buf[r0:r0 + n, ls]
            window = shift_buf[phase, 0:n, :]
            for j, a in zip(taps, offs):
                acc = acc + window[a:a + rc, :] * cw_ref[j:j + 1, ls]
        h_ref[r * rc:(r + 1) * rc, cols(c_conv + lc * LANES, LANES)] = acc

    piece = 2 * LANES
    pieces_of = lambda sections: [c0 + off for c0, width in sections for off in range(0, width, piece)]
    fillers = pieces_of(((_C_BG, D_BRANCH), (c_merge_b, D_MODEL), (_C_CG, D_BRANCH)))
    n_regions = tm // rc
    n_lc = D_BRANCH // LANES
    every = (n_regions * n_lc) // len(fillers)
    for r in range(n_regions):
        @pl.when(i >= 0)
        def _(r=r):
            after = None
            for lc in range(n_lc):
                conv_chunk(r, lc, after)
                n = r * n_lc + lc
                if (n + 1) % every == 0:
                    after = fillers[n // every]
                    proj(after, piece)
    conv_buf[0:CONV_HALO, :] = conv_buf[tm:tm + CONV_HALO, :]

    late = pieces_of(((c_merge_c, D_MODEL), (_C_Q, D_BRANCH), (_C_K, D_BRANCH), (_C_V, D_BRANCH)))
    for rs in chunks:
        h_ref[rs, cols(c_merge_a, D_MODEL)] = (
            _sigmoid(h_ref[rs, cols(c_merge_a, D_MODEL)]) * h_ref[rs, cols(c_br_a, D_MODEL)])
        if late:
            proj(late.pop(0), piece)
    for rs in chunks:
        cv = _silu(_layer_norm(h_ref[rs, cols(c_conv)] + cb_ref[...], cg_ref[...], cbeta_ref[...]))
        y_ref[rs, :] = (cv * _silu(h_ref[rs, cols(_C_BG)])).astype(BF16)
        if late:
            proj(late.pop(0), piece)
    for c0 in late:
        proj(c0, piece)

    h_ref[:, cols(c_br_b, D_MODEL)] = _dot(y_ref[...], wbr_ref[1])
    for rs in chunks:
        gc_ref[rs, :] = _silu(h_ref[rs, cols(_C_CG)]).astype(BF16)
        gate_c_ref[rs, :] = _sigmoid(h_ref[rs, cols(c_merge_c, D_MODEL)]).astype(BF16)
        k_ref[rs, :] = h_ref[rs, cols(_C_K)].astype(BF16)
    for c in range(tm // MOBA_BLOCK):
        rs = slice(c * MOBA_BLOCK, (c + 1) * MOBA_BLOCK)
        qt_ref[:, rs] = (h_ref[rs, cols(_C_Q)] * (HEAD_DIM ** -0.5 * LOG2E)).T.astype(BF16)
        vt_ref[c] = h_ref[rs, cols(_C_V)].T.astype(BF16)
        kblk = h_ref[rs, cols(_C_K)]
        kmean_ref[c] = jnp.mean(kblk, axis=0, keepdims=True)
        norm2 = _dot((kblk * kblk).astype(BF16), head_ones_ref[...])
        knorm_ref[c] = jnp.max(norm2, axis=0, keepdims=True)
    for rs in chunks:
        gated_b = _sigmoid(h_ref[rs, cols(c_merge_b, D_MODEL)]) * h_ref[rs, cols(c_br_b, D_MODEL)]
        mixed_ref[rs, :] = (h_ref[rs, cols(c_merge_a, D_MODEL)] + gated_b).astype(BF16)


def _proj_mix(x, w_in, sg_w, sg_b_bc, v_ln_g, v_ln_b, conv_w, conv_b, cv_ln_g, cv_ln_b, w_branch, head_ones,
              *, layer, tm):
    B, S, D = x.shape
    nb = S // MOBA_BLOCK
    bpt = tm // MOBA_BLOCK
    grid = (B, S // tm)
    row_spec = lambda width: pl.BlockSpec((None, tm, width), lambda b, i: (b, i, 0))
    full = lambda shape: pl.BlockSpec(shape, lambda b, i: (0,) * len(shape))
    vec = full((1, D_BRANCH))
    return pl.pallas_call(
        functools.partial(_proj_mix_kernel, tm=tm),
        grid=grid,
        in_specs=[
            row_spec(D),
            pl.BlockSpec((None, D, W_IN_COLS), lambda b, i: (layer, 0, 0), pipeline_mode=pl.Buffered(1)),
            full((GMLP_GROUPS, GMLP_CHUNK, GMLP_CHUNK)),
            full((GMLP_GROUPS, GMLP_CHUNK, GMLP_CHUNK)),
            vec, vec,
            full((CONV_WIDTH, D_BRANCH)),
            vec, vec, vec,
            pl.BlockSpec((None, 2, D_BRANCH, D_MODEL), lambda b, i: (layer, 0, 0, 0)),
            full((D_BRANCH, LANES)),
        ],
        out_specs=[row_spec(D_MODEL), row_spec(D_MODEL),
                   pl.BlockSpec((None, D_BRANCH, tm), lambda b, i: (b, 0, i)),
                   row_spec(D_BRANCH),
                   pl.BlockSpec((None, bpt, D_BRANCH, MOBA_BLOCK), lambda b, i: (b, i, 0, 0)),
                   row_spec(D_BRANCH),
                   pl.BlockSpec((None, bpt, 1, D_BRANCH), lambda b, i: (b, i, 0, 0)),
                   pl.BlockSpec((None, bpt, 1, LANES), lambda b, i: (b, i, 0, 0))],
        out_shape=[
            jax.ShapeDtypeStruct((B, S, D_MODEL), BF16),
            jax.ShapeDtypeStruct((B, S, D_MODEL), BF16),
            jax.ShapeDtypeStruct((B, D_BRANCH, S), BF16),
            jax.ShapeDtypeStruct((B, S, D_BRANCH), BF16),
            jax.ShapeDtypeStruct((B, nb, D_BRANCH, MOBA_BLOCK), BF16),
            jax.ShapeDtypeStruct((B, S, D_BRANCH), BF16),
            jax.ShapeDtypeStruct((B, nb, 1, D_BRANCH), F32),
            jax.ShapeDtypeStruct((B, nb, 1, LANES), F32),
        ],
        scratch_shapes=[pltpu.VMEM((tm, W_IN_COLS), F32),
                        pltpu.VMEM((tm, D_MODEL), BF16),
                        pltpu.VMEM((CONV_HALO + tm, D_BRANCH), F32),
                        pltpu.VMEM((SUBLANES, CONV_CHUNK + CONV_HALO, LANES), F32),
                        pltpu.VMEM((tm, D_BRANCH), BF16),
                        pltpu.VMEM((tm, D_BRANCH), BF16)],
        compiler_params=pltpu.CompilerParams(
            dimension_semantics=("arbitrary", "arbitrary"), vmem_limit_bytes=VMEM_LIMIT_BYTES),
        name="proj_mix",
    )(x, w_in, sg_w, sg_b_bc, v_ln_g, v_ln_b, conv_w, conv_b, cv_ln_g, cv_ln_b, w_branch, head_ones)


def _split_bf16(c, parts=3):
    out, rest = [], np.float32(c)
    for _ in range(parts):
        piece = np.float32(rest.astype(BF16))
        out.append(float(piece))
        rest = np.float32(rest - piece)
    return out


def _moba_kernel(qt_ref, k_ref, vt_ref, pc_ref, km_ref, kn_ref, o_ref, qx_ref, s_ref, m_ref, acc_ref):
    qb = pl.program_id(1)
    blk = MOBA_BLOCK
    nbm = MAX_BLOCKS
    rown = lax.broadcasted_iota(jnp.int32, (nbm, blk), 0)
    row_f = lax.broadcasted_iota(jnp.int32, (LANES, blk), 0)
    head_rows = (row_f < HEAD_DIM, row_f >= HEAD_DIM)
    qbv = jnp.full((nbm, blk), qb, jnp.int32)
    rowp = lax.broadcasted_iota(jnp.int32, (_PC_ROWS, blk), 0) + nbm
    t_pos = (qbv[0:_PC_ROWS, :] * blk + lax.broadcasted_iota(jnp.int32, (_PC_ROWS, blk), 1)).astype(F32)
    neg_inf = jnp.float32(-jnp.inf)
    zeros_tail = jnp.zeros((LANES - nbm - _PC_ROWS, blk), F32)
    ones_rows = jnp.ones((_SUM_ROWS, blk), BF16)

    for p in range(N_PAIRS):
        qtp = qt_ref[p * LANES:(p + 1) * LANES, :]
        km = km_ref[:, p * LANES:(p + 1) * LANES]
        km_hi = km.astype(BF16)
        km_lo = (km - km_hi.astype(F32)).astype(BF16)
        for h in range(2):
            c_parts = _split_bf16(2.0 ** (-(2 * p + h + 1) * 8.0 / N_HEADS) * LOG2E)
            qh = jnp.where(head_rows[h], qtp, jnp.zeros_like(qtp))
            gate = _dot(km_hi, qh) + _dot(km_lo, qh)
            g = jnp.where(rown < qbv, gate, neg_inf)
            selected = rown < 0
            for r in range(MOBA_TOPK):
                mx = jnp.max(g, axis=0, keepdims=True)
                idx = jnp.min(jnp.where(g == mx, rown, nbm), axis=0, keepdims=True)
                pick = rown == idx
                selected = jnp.logical_or(selected, jnp.logical_and(pick, qbv > r))
                g = jnp.where(pick, neg_inf, g)
            mask_rows = jnp.where(selected, 0.0, MASK_VALUE)
            bias_t = -sum(c_parts) * t_pos
            bias_t1 = bias_t.astype(BF16).astype(F32)
            bias_t2 = (bias_t - bias_t1).astype(BF16).astype(F32)
            pos_rows = jnp.zeros((_PC_ROWS, blk), F32)
            for i, piece in enumerate((bias_t1, bias_t2, bias_t - bias_t1 - bias_t2)):
                pos_rows = jnp.where(rowp == _PC_ONE + i, piece, pos_rows)
            for i, c_i in enumerate(c_parts):
                pos_rows = jnp.where(jnp.logical_or(rowp == _PC_BLK + i, rowp == _PC_POS + i), c_i, pos_rows)
            extra = jnp.concatenate([mask_rows, pos_rows, zeros_tail], axis=0)
            qx_ref[2 * p + h] = jnp.concatenate([qh, extra.astype(BF16)], axis=0)

    def values(vt_blk, hd):
        return jnp.concatenate([vt_blk[hd * HEAD_DIM:(hd + 1) * HEAD_DIM, :], ones_rows], axis=0)

    lane_pc = lax.broadcasted_iota(jnp.int32, (blk, LANES), 1)
    n_pairs_slots = (qb + 1) // 2
    n_slots = 2 * n_pairs_slots

    def block_rows(j):
        return pl.ds(j * blk if isinstance(j, int) else pl.multiple_of(j * blk, blk), blk)

    def pos_tile(j, final):
        pc_j = pc_ref[block_rows(j), :]
        if final is False:
            return pc_j
        drop = jnp.logical_and(jnp.full((blk, LANES), final, jnp.int32) != 0, lane_pc < nbm)
        return jnp.where(drop, jnp.zeros_like(pc_j), pc_j)

    def update(hd, s, vt_blk):
        m_old = m_ref[hd]
        m_new = jnp.maximum(m_old, jnp.max(s, axis=0, keepdims=True))
        alpha = jnp.exp2(m_old - m_new)
        pexp = jnp.exp2(s - m_new)
        acc_ref[hd] = alpha * acc_ref[hd] + _dot(values(vt_blk, hd), pexp.astype(BF16))
        m_ref[hd] = m_new

    def scores(dst, hd, j, pc_j):
        kk = jnp.concatenate([k_ref[block_rows(j), (hd // 2) * LANES:(hd // 2 + 1) * LANES], pc_j], axis=1)
        s_ref[dst, hd] = _dot(kk, qx_ref[hd])

    def phase(src, dst, vt_blk, j_next, final_next, heads):
        lead = 2
        pc_n = pos_tile(j_next, final_next)
        for step in range(len(heads) + lead):
            if step < len(heads):
                scores(dst, heads[step], j_next, pc_n)
            if step >= lead:
                update(heads[step - lead], s_ref[src, heads[step - lead]], vt_blk)

    def run_slots(t0, count, heads):
        for u in range(count):
            t_next = t0 + u + 1
            final_next = (t_next == n_slots).astype(jnp.int32)
            phase(u % 2, 1 - u % 2, vt_ref[jnp.minimum(t0 + u, qb)], jnp.minimum(t_next, qb), final_next, heads)

    unroll = 4
    key_dist = ((qbv - rown) * blk + lax.broadcasted_iota(jnp.int32, (nbm, blk), 1) - (blk - 1)).astype(F32)
    kt_own = {}
    join_at = []
    for heads, window in FAR_SKIP_CLASSES:
        t_max = jnp.maximum(qb - window, 0) // unroll * unroll
        class_ok = None
        for hd in heads:
            p, h = divmod(hd, 2)
            if p not in kt_own:
                kt_own[p] = k_ref[block_rows(qb), p * LANES:(p + 1) * LANES].astype(F32).T
            c_hd = 2.0 ** (-(hd + 1) * 8.0 / N_HEADS) * LOG2E
            qf = qt_ref[hd * HEAD_DIM:(hd + 1) * HEAD_DIM, :].astype(F32)
            q_norm = jnp.sqrt(jnp.sum(qf * qf, axis=0, keepdims=True))
            diag = jnp.sum(qf * kt_own[p][h * HEAD_DIM:(h + 1) * HEAD_DIM, :], axis=0, keepdims=True)
            k_norm = jnp.sqrt(kn_ref[:, hd:hd + 1])
            slack = 0.02 * q_norm * jnp.max(k_norm, axis=0, keepdims=True) + 1.0
            bound = 1.02 * k_norm * q_norm - c_hd * key_dist - diag + slack
            masked = jnp.where(rown < t_max, bound, neg_inf)
            worst = jnp.max(jnp.max(masked, axis=0, keepdims=True), axis=1, keepdims=True)[0, 0]
            ok = worst < FAR_SKIP_LOG2
            class_ok = ok if class_ok is None else jnp.logical_and(class_ok, ok)
        join_at.append(jnp.where(class_ok, t_max, 0))
    for c in range(len(join_at) - 2, -1, -1):
        join_at[c] = jnp.minimum(join_at[c], join_at[c + 1])

    m_ref[...] = jnp.full(m_ref.shape, -jnp.inf, F32)
    acc_ref[...] = jnp.zeros(acc_ref.shape, F32)
    first_final = (qb == 0).astype(jnp.int32)
    active = tuple(hd for hd in range(N_HEADS) if all(hd not in heads for heads, _ in FAR_SKIP_CLASSES))
    pc_0 = pos_tile(0, first_final)
    for hd in active:
        scores(0, hd, 0, pc_0)

    def slot_loop(t_from, t_to, heads):
        def trip(i, carry):
            run_slots(t_from + unroll * i, unroll, heads)
            return carry
        lax.fori_loop(0, (t_to - t_from) // unroll, trip, 0)

    t_prev = 0
    for (heads, _), t_join in zip(FAR_SKIP_CLASSES, join_at):
        slot_loop(t_prev, t_join, active)
        pc_join = pos_tile(jnp.minimum(t_join, qb), jnp.where(t_join == 0, first_final, 0))
        for hd in heads:
            scores(0, hd, jnp.minimum(t_join, qb), pc_join)
        active, t_prev = active + heads, t_join
    slot_loop(t_prev, n_slots, active)

    @pl.when((n_slots - t_prev) % unroll != 0)
    def _():
        run_slots(n_slots - 2, 2, active)

    key_i = lax.broadcasted_iota(jnp.int32, (blk, blk), 0)
    qry_i = lax.broadcasted_iota(jnp.int32, (blk, blk), 1)
    vt_own = vt_ref[qb]
    for hd in range(N_HEADS):
        update(hd, jnp.where(key_i <= qry_i, s_ref[0, hd], MASK_VALUE), vt_own)

    outs = []
    for hd in range(N_HEADS):
        acc = acc_ref[hd]
        outs.append(acc[0:HEAD_DIM, :] / acc[HEAD_DIM:HEAD_DIM + 1, :])
    o_ref[...] = jnp.concatenate(outs, axis=0).T.astype(o_ref.dtype)


def _moba(qt, k, vt, kmean, knorm, pos_const):
    B, S, _ = k.shape
    nb = S // MOBA_BLOCK
    return pl.pallas_call(
        _moba_kernel,
        grid=(B, nb),
        in_specs=[
            pl.BlockSpec((None, D_BRANCH, MOBA_BLOCK), lambda b, i: (b, 0, i)),
            pl.BlockSpec((None, S, D_BRANCH), lambda b, i: (b, 0, 0)),
            pl.BlockSpec((None, nb, D_BRANCH, MOBA_BLOCK), lambda b, i: (b, 0, 0, 0)),
            pl.BlockSpec((S, LANES), lambda b, i: (0, 0)),
            pl.BlockSpec((None, MAX_BLOCKS, D_BRANCH), lambda b, i: (b, 0, 0)),
            pl.BlockSpec((None, MAX_BLOCKS, LANES), lambda b, i: (b, 0, 0)),
        ],
        out_specs=pl.BlockSpec((None, MOBA_BLOCK, D_BRANCH), lambda b, i: (b, i, 0)),
        out_shape=jax.ShapeDtypeStruct((B, S, D_BRANCH), BF16),
        scratch_shapes=[
            pltpu.VMEM((N_HEADS, 2 * LANES, MOBA_BLOCK), BF16),
            pltpu.VMEM((2, N_HEADS, MOBA_BLOCK, MOBA_BLOCK), F32),
            pltpu.VMEM((N_HEADS, 1, MOBA_BLOCK), F32),
            pltpu.VMEM((N_HEADS, HEAD_DIM + _SUM_ROWS, MOBA_BLOCK), F32),
        ],
        compiler_params=pltpu.CompilerParams(
            dimension_semantics=("arbitrary", "arbitrary"), vmem_limit_bytes=VMEM_LIMIT_BYTES),
        name="moba",
    )(qt, k, vt, pos_const, kmean, knorm)


def _merge_out_kernel(x_ref, mixed_ref, gate_c_ref, att_ref, gc_ref, wbc_ref, wout_ref, g_ref, b_ref,
                      o_ref, *, alpha):
    y_c = (att_ref[...].astype(F32) * gc_ref[...].astype(F32)).astype(BF16)
    mixed = mixed_ref[...].astype(F32) + gate_c_ref[...].astype(F32) * _dot(y_c, wbc_ref[...])
    out = _dot(mixed.astype(BF16), wout_ref[...])
    o_ref[...] = _layer_norm(alpha * x_ref[...] + out, g_ref[...], b_ref[...])


def _merge_out(x, mixed_ab, gate_c, att, gc, w_branch, w_out, ln_g, ln_b, *, layer, tm, alpha):
    B, S, D = x.shape
    row_spec = lambda width: pl.BlockSpec((None, tm, width), lambda b, i: (b, i, 0))
    full = lambda shape: pl.BlockSpec(shape, lambda b, i: (0,) * len(shape))
    return pl.pallas_call(
        functools.partial(_merge_out_kernel, alpha=alpha),
        grid=(B, S // tm),
        in_specs=[row_spec(D), row_spec(D_MODEL), row_spec(D_MODEL), row_spec(D_BRANCH), row_spec(D_BRANCH),
                  pl.BlockSpec((None, None, D_BRANCH, D_MODEL), lambda b, i: (layer, 2, 0, 0)),
                  pl.BlockSpec((None, D_MODEL, D_MODEL), lambda b, i: (layer, 0, 0)),
                  full((1, D_MODEL)), full((1, D_MODEL))],
        out_specs=row_spec(D_MODEL),
        out_shape=jax.ShapeDtypeStruct((B, S, D_MODEL), F32),
        compiler_params=pltpu.CompilerParams(
            dimension_semantics=("arbitrary", "arbitrary"), vmem_limit_bytes=VMEM_LIMIT_BYTES),
        name="merge_out",
    )(x, mixed_ab, gate_c, att, gc, w_branch, w_out, ln_g, ln_b)


def _position_constants(S):
    s = jnp.arange(S, dtype=jnp.int32)
    blk, pos = s // MOBA_BLOCK, s % MOBA_BLOCK
    lane = jnp.arange(LANES, dtype=jnp.int32)[None, :]
    pc = (lane == blk[:, None]).astype(F32)
    pc = jnp.where((lane >= _PC_ONE) & (lane < _PC_ONE + 3), 1.0, pc)
    pc = jnp.where((lane >= _PC_BLK) & (lane < _PC_BLK + 3), (blk * MOBA_BLOCK).astype(F32)[:, None], pc)
    pc = jnp.where((lane >= _PC_POS) & (lane < _PC_POS + 3), pos.astype(F32)[:, None], pc)
    return pc.astype(BF16)


def kernel(x, w_in, sg_w, sg_b, v_ln_g, v_ln_b, conv_w, conv_b, cv_ln_g, cv_ln_b, w_branch, w_out, ln_g, ln_b):
    depth = w_in.shape[0]
    B, S, D = x.shape
    nb = S // MOBA_BLOCK
    assert D == D_MODEL and S % MOBA_BLOCK == 0 and nb <= MAX_BLOCKS == _PC_ONE
    alpha = (2 * depth) ** 0.25
    tm = 512
    pos_const = _position_constants(S)
    w_in_b = w_in.astype(BF16)
    w_br_b = w_branch.astype(BF16)
    w_out_b = w_out.astype(BF16)
    head_ones = (jnp.arange(D_BRANCH)[:, None] // HEAD_DIM == jnp.arange(LANES)[None, :]).astype(BF16)
    sg_b_bc = jnp.broadcast_to(sg_b[..., None], sg_b.shape + (D_BRANCH // GMLP_GROUPS,))
    row = lambda a, l: a[l][None, :]
    for l in range(depth):
        mixed_ab, gate_c, qt, k, vt, gc, kmean, knorm = _proj_mix(
            x, w_in_b, sg_w[l], sg_b_bc[l], row(v_ln_g, l), row(v_ln_b, l), conv_w[l], row(conv_b, l),
            row(cv_ln_g, l), row(cv_ln_b, l), w_br_b, head_ones, layer=l, tm=tm)
        pad_blocks = lambda a: jnp.pad(a.reshape(B, nb, -1), ((0, 0), (0, MAX_BLOCKS - nb), (0, 0)))
        att = _moba(qt, k, vt, pad_blocks(kmean), pad_blocks(knorm), pos_const)
        x = _merge_out(x, mixed_ab, gate_c, att, gc, w_br_b, w_out_b, row(ln_g, l), row(ln_b, l),
                       layer=l, tm=tm, alpha=alpha)
    return x
```

```python
import functools

import numpy as np
import jax
import jax.numpy as jnp
from jax import lax
from jax.experimental import pallas as pl
from jax.experimental.pallas import tpu as pltpu

F32 = jnp.float32
BF16 = jnp.bfloat16

D_MODEL = 1024
D_BRANCH = 512
GMLP_GROUPS = 4
GMLP_CHUNK = 128
CONV_WIDTH = 31
CONV_HALO = 32
CONV_CHUNK = 128
ROW_CHUNK = 64
HEAD_DIM = 64
N_HEADS = 8
MOBA_BLOCK = 256
MOBA_TOPK = 3
N_BRANCH = 3
W_IN_COLS = 10 * D_BRANCH + N_BRANCH * D_MODEL
LN_EPS = 1e-5
MASK_VALUE = -1e30
LANES = 128
SUBLANES = 8
LOG2E = 1.4426950408889634
N_PAIRS = N_HEADS * HEAD_DIM // LANES
MAX_BLOCKS = 32
VMEM_LIMIT_BYTES = 56 * 1024 * 1024
FAR_SKIP_CLASSES = (((4,), 12), ((2, 3), 6), ((0, 1), 2))
FAR_SKIP_LOG2 = -100.0

_C_AU, _C_AV, _C_AG = 0, 512, 1024
_C_BVAL, _C_BGLU, _C_BG = 1536, 2048, 2560
_C_Q, _C_K, _C_V, _C_CG = 3072, 3584, 4096, 4608
_C_MERGE = 5120

_PC_ONE, _PC_BLK, _PC_POS = 32, 35, 38
_PC_ROWS = 16
_SUM_ROWS = 16


def _sigmoid(x):
    return 0.5 * jnp.tanh(0.5 * x) + 0.5


def _silu(x):
    return x * _sigmoid(x)


def _layer_norm(x, g, b):
    mu = jnp.mean(x, axis=-1, keepdims=True)
    xc = x - mu
    var = jnp.mean(xc * xc, axis=-1, keepdims=True)
    return xc * lax.rsqrt(var + LN_EPS) * g + b


def _dot(a, b):
    return jnp.dot(a, b, preferred_element_type=F32)


def _proj_mix_kernel(x_ref, w_in_ref, sg_w_ref, sg_b_ref, vg_ref, vb_ref, cw_ref, cb_ref,
                     cg_ref, cbeta_ref, wbr_ref, head_ones_ref,
                     mixed_ref, gate_c_ref, qt_ref, k_ref, vt_ref, gc_ref, kmean_ref, knorm_ref,
                     h_ref, xb_ref, conv_buf, shift_buf, vn_ref, y_ref, *, tm):
    i = pl.program_id(1)
    xb_ref[...] = x_ref[...].astype(BF16)
    chunks = [slice(r * ROW_CHUNK, (r + 1) * ROW_CHUNK) for r in range(tm // ROW_CHUNK)]
    cols = lambda c0, width=D_BRANCH: slice(c0, c0 + width)
    c_merge_a, c_merge_b, c_merge_c = _C_MERGE, _C_MERGE + D_MODEL, _C_MERGE + 2 * D_MODEL
    c_z = _C_AV
    c_br_a = _C_AU
    c_conv = _C_BVAL
    c_br_b = _C_AU

    def proj(c0, width=D_BRANCH):
        h_ref[:, cols(c0, width)] = _dot(xb_ref[...], w_in_ref[:, cols(c0, width)])

    proj(_C_AV)
    for rs in chunks:
        vn_ref[rs, :] = _layer_norm(h_ref[rs, cols(_C_AV)], vg_ref[...], vb_ref[...]).astype(BF16)
    proj(_C_AU)
    proj(_C_AG)
    proj(c_merge_a, D_MODEL)
    row = lax.broadcasted_iota(jnp.int32, (GMLP_CHUNK, GMLP_CHUNK), 0)
    col = lax.broadcasted_iota(jnp.int32, (GMLP_CHUNK, GMLP_CHUNK), 1)
    cgw = D_BRANCH // GMLP_GROUPS
    for g in range(GMLP_GROUPS):
        wm = jnp.where(col <= row, sg_w_ref[g], 0.0).astype(BF16)
        for c in range(tm // GMLP_CHUNK):
            rs = slice(c * GMLP_CHUNK, (c + 1) * GMLP_CHUNK)
            h_ref[rs, cols(c_z + g * cgw, cgw)] = _dot(wm, vn_ref[rs, cols(g * cgw, cgw)]) + sg_b_ref[g]
    for rs in chunks:
        y_a = h_ref[rs, cols(_C_AU)] * h_ref[rs, cols(c_z)] * _silu(h_ref[rs, cols(_C_AG)])
        y_ref[rs, :] = y_a.astype(BF16)

    proj(_C_BVAL)
    proj(_C_BGLU)

    @pl.when(i == 0)
    def _():
        conv_buf[0:CONV_HALO, :] = jnp.zeros((CONV_HALO, D_BRANCH), F32)

    for rs in chunks:
        glu = h_ref[rs, cols(_C_BVAL)] * _sigmoid(h_ref[rs, cols(_C_BGLU)])
        conv_buf[CONV_HALO + rs.start:CONV_HALO + rs.stop, :] = glu
    h_ref[:, cols(c_br_a, D_MODEL)] = _dot(y_ref[...], wbr_ref[0])
    rc = CONV_CHUNK
    first = CONV_HALO - (CONV_WIDTH - 1)

    def zero_after(c0):
        bits = pltpu.bitcast(h_ref[0:SUBLANES, cols(c0, LANES)], jnp.uint32)
        return pltpu.bitcast(lax.shift_right_logical(lax.shift_right_logical(bits, jnp.uint32(16)), jnp.uint32(16)),
                             F32)

    def conv_chunk(r, lc, after=None):
        ls = slice(lc * LANES, (lc + 1) * LANES)
        acc = jnp.zeros((rc, LANES), F32)
        if after is not None:
            acc = acc + jnp.concatenate([zero_after(after)] * (rc // SUBLANES), axis=0)
        for phase in range(SUBLANES):
            taps = [j for j in range(CONV_WIDTH) if (first + j) % SUBLANES == phase]
            if not taps:
                continue
            offs = [(first + j) // SUBLANES * SUBLANES for j in taps]
            r0 = r * rc + phase
            n = rc + max(offs)
            shift_buf[phase, 0:n, :] = conv_buf[r0:r0 + n, ls]
            window = shift_buf[phase, 0:n, :]
            for j, a in zip(taps, offs):
                acc = acc + window[a:a + rc, :] * cw_ref[j:j + 1, ls]
        h_ref[r * rc:(r + 1) * rc, cols(c_conv + lc * LANES, LANES)] = acc

    piece = 2 * LANES
    pieces_of = lambda sections: [c0 + off for c0, width in sections for off in range(0, width, piece)]
    fillers = pieces_of(((_C_BG, D_BRANCH), (c_merge_b, D_MODEL), (_C_CG, D_BRANCH)))
    n_regions = tm // rc
    n_lc = D_BRANCH // LANES
    every = (n_regions * n_lc) // len(fillers)
    for r in range(n_regions):
        @pl.when(i >= 0)
        def _(r=r):
            after = None
            for lc in range(n_lc):
                conv_chunk(r, lc, after)
                n = r * n_lc + lc
                if (n + 1) % every == 0:
                    after = fillers[n // every]
                    proj(after, piece)
    conv_buf[0:CONV_HALO, :] = conv_buf[tm:tm + CONV_HALO, :]

    late = pieces_of(((c_merge_c, D_MODEL), (_C_Q, D_BRANCH), (_C_K, D_BRANCH), (_C_V, D_BRANCH)))
    for rs in chunks:
        h_ref[rs, cols(c_merge_a, D_MODEL)] = (
            _sigmoid(h_ref[rs, cols(c_merge_a, D_MODEL)]) * h_ref[rs, cols(c_br_a, D_MODEL)])
        if late:
            proj(late.pop(0), piece)
    for rs in chunks:
        cv = _silu(_layer_norm(h_ref[rs, cols(c_conv)] + cb_ref[...], cg_ref[...], cbeta_ref[...]))
        y_ref[rs, :] = (cv * _silu(h_ref[rs, cols(_C_BG)])).astype(BF16)
        if late:
            proj(late.pop(0), piece)
    for c0 in late:
        proj(c0, piece)

    h_ref[:, cols(c_br_b, D_MODEL)] = _dot(y_ref[...], wbr_ref[1])
    for rs in chunks:
        gc_ref[rs, :] = _silu(h_ref[rs, cols(_C_CG)]).astype(BF16)
        gate_c_ref[rs, :] = _sigmoid(h_ref[rs, cols(c_merge_c, D_MODEL)]).astype(BF16)
        k_ref[rs, :] = h_ref[rs, cols(_C_K)].astype(BF16)
    for c in range(tm // MOBA_BLOCK):
        rs = slice(c * MOBA_BLOCK, (c + 1) * MOBA_BLOCK)
        qt_ref[:, rs] = (h_ref[rs, cols(_C_Q)] * (HEAD_DIM ** -0.5 * LOG2E)).T.astype(BF16)
        vt_ref[c] = h_ref[rs, cols(_C_V)].T.astype(BF16)
        kblk = h_ref[rs, cols(_C_K)]
        kmean_ref[c] = jnp.mean(kblk, axis=0, keepdims=True)
        norm2 = _dot((kblk * kblk).astype(BF16), head_ones_ref[...])
        knorm_ref[c] = jnp.max(norm2, axis=0, keepdims=True)
    for rs in chunks:
        gated_b = _sigmoid(h_ref[rs, cols(c_merge_b, D_MODEL)]) * h_ref[rs, cols(c_br_b, D_MODEL)]
        mixed_ref[rs, :] = (h_ref[rs, cols(c_merge_a, D_MODEL)] + gated_b).astype(BF16)


def _proj_mix(x, w_in, sg_w, sg_b_bc, v_ln_g, v_ln_b, conv_w, conv_b, cv_ln_g, cv_ln_b, w_branch, head_ones,
              *, layer, tm):
    B, S, D = x.shape
    nb = S // MOBA_BLOCK
    bpt = tm // MOBA_BLOCK
    grid = (B, S // tm)
    row_spec = lambda width: pl.BlockSpec((None, tm, width), lambda b, i: (b, i, 0))
    full = lambda shape: pl.BlockSpec(shape, lambda b, i: (0,) * len(shape))
    vec = full((1, D_BRANCH))
    return pl.pallas_call(
        functools.partial(_proj_mix_kernel, tm=tm),
        grid=grid,
        in_specs=[
            row_spec(D),
            pl.BlockSpec((None, D, W_IN_COLS), lambda b, i: (layer, 0, 0), pipeline_mode=pl.Buffered(1)),
            full((GMLP_GROUPS, GMLP_CHUNK, GMLP_CHUNK)),
            full((GMLP_GROUPS, GMLP_CHUNK, GMLP_CHUNK)),
            vec, vec,
            full((CONV_WIDTH, D_BRANCH)),
            vec, vec, vec,
            pl.BlockSpec((None, 2, D_BRANCH, D_MODEL), lambda b, i: (layer, 0, 0, 0)),
            full((D_BRANCH, LANES)),
        ],
        out_specs=[row_spec(D_MODEL), row_spec(D_MODEL),
                   pl.BlockSpec((None, D_BRANCH, tm), lambda b, i: (b, 0, i)),
                   row_spec(D_BRANCH),
                   pl.BlockSpec((None, bpt, D_BRANCH, MOBA_BLOCK), lambda b, i: (b, i, 0, 0)),
                   row_spec(D_BRANCH),
                   pl.BlockSpec((None, bpt, 1, D_BRANCH), lambda b, i: (b, i, 0, 0)),
                   pl.BlockSpec((None, bpt, 1, LANES), lambda b, i: (b, i, 0, 0))],
        out_shape=[
            jax.ShapeDtypeStruct((B, S, D_MODEL), BF16),
            jax.ShapeDtypeStruct((B, S, D_MODEL), BF16),
            jax.ShapeDtypeStruct((B, D_BRANCH, S), BF16),
            jax.ShapeDtypeStruct((B, S, D_BRANCH), BF16),
            jax.ShapeDtypeStruct((B, nb, D_BRANCH, MOBA_BLOCK), BF16),
            jax.ShapeDtypeStruct((B, S, D_BRANCH), BF16),
            jax.ShapeDtypeStruct((B, nb, 1, D_BRANCH), F32),
            jax.ShapeDtypeStruct((B, nb, 1, LANES), F32),
        ],
        scratch_shapes=[pltpu.VMEM((tm, W_IN_COLS), F32),
                        pltpu.VMEM((tm, D_MODEL), BF16),
                        pltpu.VMEM((CONV_HALO + tm, D_BRANCH), F32),
                        pltpu.VMEM((SUBLANES, CONV_CHUNK + CONV_HALO, LANES), F32),
                        pltpu.VMEM((tm, D_BRANCH), BF16),
                        pltpu.VMEM((tm, D_BRANCH), BF16)],
        compiler_params=pltpu.CompilerParams(
            dimension_semantics=("arbitrary", "arbitrary"), vmem_limit_bytes=VMEM_LIMIT_BYTES),
        name="proj_mix",
    )(x, w_in, sg_w, sg_b_bc, v_ln_g, v_ln_b, conv_w, conv_b, cv_ln_g, cv_ln_b, w_branch, head_ones)


def _split_bf16(c, parts=3):
    out, rest = [], np.float32(c)
    for _ in range(parts):
        piece = np.float32(rest.astype(BF16))
        out.append(float(piece))
        rest = np.float32(rest - piece)
    return out


def _moba_kernel(qt_ref, k_ref, vt_ref, pc_ref, km_ref, kn_ref, o_ref, qx_ref, s_ref, m_ref, acc_ref):
    qb = pl.program_id(1)
    blk = MOBA_BLOCK
    nbm = MAX_BLOCKS
    rown = lax.broadcasted_iota(jnp.int32, (nbm, blk), 0)
    row_f = lax.broadcasted_iota(jnp.int32, (LANES, blk), 0)
    head_rows = (row_f < HEAD_DIM, row_f >= HEAD_DIM)
    qbv = jnp.full((nbm, blk), qb, jnp.int32)
    rowp = lax.broadcasted_iota(jnp.int32, (_PC_ROWS, blk), 0) + nbm
    t_pos = (qbv[0:_PC_ROWS, :] * blk + lax.broadcasted_iota(jnp.int32, (_PC_ROWS, blk), 1)).astype(F32)
    neg_inf = jnp.float32(-jnp.inf)
    zeros_tail = jnp.zeros((LANES - nbm - _PC_ROWS, blk), F32)
    ones_rows = jnp.ones((_SUM_ROWS, blk), BF16)

    for p in range(N_PAIRS):
        qtp = qt_ref[p * LANES:(p + 1) * LANES, :]
        km = km_ref[:, p * LANES:(p + 1) * LANES]
        km_hi = km.astype(BF16)
        km_lo = (km - km_hi.astype(F32)).astype(BF16)
        for h in range(2):
            c_parts = _split_bf16(2.0 ** (-(2 * p + h + 1) * 8.0 / N_HEADS) * LOG2E)
            qh = jnp.where(head_rows[h], qtp, jnp.zeros_like(qtp))
            gate = _dot(km_hi, qh) + _dot(km_lo, qh)
            g = jnp.where(rown < qbv, gate, neg_inf)
            selected = rown < 0
            for r in range(MOBA_TOPK):
                mx = jnp.max(g, axis=0, keepdims=True)
                idx = jnp.min(jnp.where(g == mx, rown, nbm), axis=0, keepdims=True)
                pick = rown == idx
                selected = jnp.logical_or(selected, jnp.logical_and(pick, qbv > r))
                g = jnp.where(pick, neg_inf, g)
            mask_rows = jnp.where(selected, 0.0, MASK_VALUE)
            bias_t = -sum(c_parts) * t_pos
            bias_t1 = bias_t.astype(BF16).astype(F32)
            bias_t2 = (bias_t - bias_t1).astype(BF16).astype(F32)
            pos_rows = jnp.zeros((_PC_ROWS, blk), F32)
            for i, piece in enumerate((bias_t1, bias_t2, bias_t - bias_t1 - bias_t2)):
                pos_rows = jnp.where(rowp == _PC_ONE + i, piece, pos_rows)
            for i, c_i in enumerate(c_parts):
                pos_rows = jnp.where(jnp.logical_or(rowp == _PC_BLK + i, rowp == _PC_POS + i), c_i, pos_rows)
            extra = jnp.concatenate([mask_rows, pos_rows, zeros_tail], axis=0)
            qx_ref[2 * p + h] = jnp.concatenate([qh, extra.astype(BF16)], axis=0)

    def values(vt_blk, hd):
        return jnp.concatenate([vt_blk[hd * HEAD_DIM:(hd + 1) * HEAD_DIM, :], ones_rows], axis=0)

    lane_pc = lax.broadcasted_iota(jnp.int32, (blk, LANES), 1)
    n_slots = qb // 2 * 2

    def block_rows(j):
        return pl.ds(j * blk if isinstance(j, int) else pl.multiple_of(j * blk, blk), blk)

    def pos_tile(j, final):
        pc_j = pc_ref[block_rows(j), :]
        if final is False:
            return pc_j
        drop = jnp.logical_and(jnp.full((blk, LANES), final, jnp.int32) != 0, lane_pc < nbm)
        return jnp.where(drop, jnp.zeros_like(pc_j), pc_j)

    def update(hd, s, vt_blk):
        m_old = m_ref[hd]
        m_new = jnp.maximum(m_old, jnp.max(s, axis=0, keepdims=True))
        alpha = jnp.exp2(m_old - m_new)
        pexp = jnp.exp2(s - m_new)
        acc_ref[hd] = alpha * acc_ref[hd] + _dot(values(vt_blk, hd), pexp.astype(BF16))
        m_ref[hd] = m_new

    def scores(dst, hd, j, pc_j):
        kk = jnp.concatenate([k_ref[block_rows(j), (hd // 2) * LANES:(hd // 2 + 1) * LANES], pc_j], axis=1)
        s_ref[dst, hd] = _dot(kk, qx_ref[hd])

    def phase(src, dst, vt_blk, j_next, final_next, heads):
        lead = 2
        pc_n = pos_tile(j_next, final_next)
        for step in range(len(heads) + lead):
            if step < len(heads):
                scores(dst, heads[step], j_next, pc_n)
            if step >= lead:
                update(heads[step - lead], s_ref[src, heads[step - lead]], vt_blk)

    def run_slots(t0, count, heads):
        for u in range(count):
            t_next = t0 + u + 1
            final_next = (t_next == qb).astype(jnp.int32)
            phase(u % 2, 1 - u % 2, vt_ref[t0 + u], t_next, final_next, heads)

    unroll = 4
    key_dist = ((qbv - rown) * blk + lax.broadcasted_iota(jnp.int32, (nbm, blk), 1) - (blk - 1)).astype(F32)
    kt_own = {}
    join_at = []
    for heads, window in FAR_SKIP_CLASSES:
        t_max = jnp.maximum(qb - window, 0) // unroll * unroll
        class_ok = None
        for hd in heads:
            p, h = divmod(hd, 2)
            if p not in kt_own:
                kt_own[p] = k_ref[block_rows(qb), p * LANES:(p + 1) * LANES].astype(F32).T
            c_hd = 2.0 ** (-(hd + 1) * 8.0 / N_HEADS) * LOG2E
            qf = qt_ref[hd * HEAD_DIM:(hd + 1) * HEAD_DIM, :].astype(F32)
            q_norm = jnp.sqrt(jnp.sum(qf * qf, axis=0, keepdims=True))
            diag = jnp.sum(qf * kt_own[p][h * HEAD_DIM:(h + 1) * HEAD_DIM, :], axis=0, keepdims=True)
            k_norm = jnp.sqrt(kn_ref[:, hd:hd + 1])
            slack = 0.02 * q_norm * jnp.max(k_norm, axis=0, keepdims=True) + 1.0
            bound = 1.02 * k_norm * q_norm - c_hd * key_dist - diag + slack
            masked = jnp.where(rown < t_max, bound, neg_inf)
            worst = jnp.max(jnp.max(masked, axis=0, keepdims=True), axis=1, keepdims=True)[0, 0]
            ok = worst < FAR_SKIP_LOG2
            class_ok = ok if class_ok is None else jnp.logical_and(class_ok, ok)
        join_at.append(jnp.where(class_ok, t_max, 0))
    for c in range(len(join_at) - 2, -1, -1):
        join_at[c] = jnp.minimum(join_at[c], join_at[c + 1])

    m_ref[...] = jnp.full(m_ref.shape, -jnp.inf, F32)
    acc_ref[...] = jnp.zeros(acc_ref.shape, F32)
    first_final = (qb == 0).astype(jnp.int32)
    active = tuple(hd for hd in range(N_HEADS) if all(hd not in heads for heads, _ in FAR_SKIP_CLASSES))
    pc_0 = pos_tile(0, first_final)
    for hd in active:
        scores(0, hd, 0, pc_0)

    def slot_loop(t_from, t_to, heads):
        def trip(i, carry):
            run_slots(t_from + unroll * i, unroll, heads)
            return carry
        lax.fori_loop(0, (t_to - t_from) // unroll, trip, 0)

    t_prev = 0
    for (heads, _), t_join in zip(FAR_SKIP_CLASSES, join_at):
        slot_loop(t_prev, t_join, active)
        pc_join = pos_tile(jnp.minimum(t_join, qb), jnp.where(t_join == 0, first_final, 0))
        for hd in heads:
            scores(0, hd, jnp.minimum(t_join, qb), pc_join)
        active, t_prev = active + heads, t_join
    slot_loop(t_prev, n_slots, active)

    @pl.when((n_slots - t_prev) % unroll != 0)
    def _():
        run_slots(n_slots - 2, 2, active)

    def finish(buf):
        key_i = lax.broadcasted_iota(jnp.int32, (blk, blk), 0)
        qry_i = lax.broadcasted_iota(jnp.int32, (blk, blk), 1)
        vt_own = vt_ref[qb]
        for hd in range(N_HEADS):
            update(hd, jnp.where(key_i <= qry_i, s_ref[buf, hd], MASK_VALUE), vt_own)
        outs = []
        for hd in range(N_HEADS):
            acc = acc_ref[hd]
            outs.append(acc[0:HEAD_DIM, :] / acc[HEAD_DIM:HEAD_DIM + 1, :])
        o_ref[...] = jnp.concatenate(outs, axis=0).T.astype(o_ref.dtype)

    @pl.when(qb % 2 == 1)
    def _():
        phase(0, 1, vt_ref[qb - 1], qb, 1, active)
        finish(1)

    @pl.when(qb % 2 == 0)
    def _():
        finish(0)


def _moba(qt, k, vt, kmean, knorm, pos_const):
    B, S, _ = k.shape
    nb = S // MOBA_BLOCK
    return pl.pallas_call(
        _moba_kernel,
        grid=(B, nb),
        in_specs=[
            pl.BlockSpec((None, D_BRANCH, MOBA_BLOCK), lambda b, i: (b, 0, i)),
            pl.BlockSpec((None, S, D_BRANCH), lambda b, i: (b, 0, 0)),
            pl.BlockSpec((None, nb, D_BRANCH, MOBA_BLOCK), lambda b, i: (b, 0, 0, 0)),
            pl.BlockSpec((S, LANES), lambda b, i: (0, 0)),
            pl.BlockSpec((None, MAX_BLOCKS, D_BRANCH), lambda b, i: (b, 0, 0)),
            pl.BlockSpec((None, MAX_BLOCKS, LANES), lambda b, i: (b, 0, 0)),
        ],
        out_specs=pl.BlockSpec((None, MOBA_BLOCK, D_BRANCH), lambda b, i: (b, i, 0)),
        out_shape=jax.ShapeDtypeStruct((B, S, D_BRANCH), BF16),
        scratch_shapes=[
            pltpu.VMEM((N_HEADS, 2 * LANES, MOBA_BLOCK), BF16),
            pltpu.VMEM((2, N_HEADS, MOBA_BLOCK, MOBA_BLOCK), F32),
            pltpu.VMEM((N_HEADS, 1, MOBA_BLOCK), F32),
            pltpu.VMEM((N_HEADS, HEAD_DIM + _SUM_ROWS, MOBA_BLOCK), F32),
        ],
        compiler_params=pltpu.CompilerParams(
            dimension_semantics=("arbitrary", "arbitrary"), vmem_limit_bytes=VMEM_LIMIT_BYTES),
        name="moba",
    )(qt, k, vt, pos_const, kmean, knorm)


def _merge_out_kernel(x_ref, mixed_ref, gate_c_ref, att_ref, gc_ref, wbc_ref, wout_ref, g_ref, b_ref,
                      o_ref, *, alpha):
    y_c = (att_ref[...].astype(F32) * gc_ref[...].astype(F32)).astype(BF16)
    mixed = mixed_ref[...].astype(F32) + gate_c_ref[...].astype(F32) * _dot(y_c, wbc_ref[...])
    out = _dot(mixed.astype(BF16), wout_ref[...])
    o_ref[...] = _layer_norm(alpha * x_ref[...] + out, g_ref[...], b_ref[...])


def _merge_out(x, mixed_ab, gate_c, att, gc, w_branch, w_out, ln_g, ln_b, *, layer, tm, alpha):
    B, S, D = x.shape
    row_spec = lambda width: pl.BlockSpec((None, tm, width), lambda b, i: (b, i, 0))
    full = lambda shape: pl.BlockSpec(shape, lambda b, i: (0,) * len(shape))
    return pl.pallas_call(
        functools.partial(_merge_out_kernel, alpha=alpha),
        grid=(B, S // tm),
        in_specs=[row_spec(D), row_spec(D_MODEL), row_spec(D_MODEL), row_spec(D_BRANCH), row_spec(D_BRANCH),
                  pl.BlockSpec((None, None, D_BRANCH, D_MODEL), lambda b, i: (layer, 2, 0, 0)),
                  pl.BlockSpec((None, D_MODEL, D_MODEL), lambda b, i: (layer, 0, 0)),
                  full((1, D_MODEL)), full((1, D_MODEL))],
        out_specs=row_spec(D_MODEL),
        out_shape=jax.ShapeDtypeStruct((B, S, D_MODEL), F32),
        compiler_params=pltpu.CompilerParams(
            dimension_semantics=("arbitrary", "arbitrary"), vmem_limit_bytes=VMEM_LIMIT_BYTES),
        name="merge_out",
    )(x, mixed_ab, gate_c, att, gc, w_branch, w_out, ln_g, ln_b)


def _position_constants(S):
    s = jnp.arange(S, dtype=jnp.int32)
    blk, pos = s // MOBA_BLOCK, s % MOBA_BLOCK
    lane = jnp.arange(LANES, dtype=jnp.int32)[None, :]
    pc = (lane == blk[:, None]).astype(F32)
    pc = jnp.where((lane >= _PC_ONE) & (lane < _PC_ONE + 3), 1.0, pc)
    pc = jnp.where((lane >= _PC_BLK) & (lane < _PC_BLK + 3), (blk * MOBA_BLOCK).astype(F32)[:, None], pc)
    pc = jnp.where((lane >= _PC_POS) & (lane < _PC_POS + 3), pos.astype(F32)[:, None], pc)
    return pc.astype(BF16)


def kernel(x, w_in, sg_w, sg_b, v_ln_g, v_ln_b, conv_w, conv_b, cv_ln_g, cv_ln_b, w_branch, w_out, ln_g, ln_b):
    depth = w_in.shape[0]
    B, S, D = x.shape
    nb = S // MOBA_BLOCK
    assert D == D_MODEL and S % MOBA_BLOCK == 0 and nb <= MAX_BLOCKS == _PC_ONE
    alpha = (2 * depth) ** 0.25
    tm = 512
    pos_const = _position_constants(S)
    w_in_b = w_in.astype(BF16)
    w_br_b = w_branch.astype(BF16)
    w_out_b = w_out.astype(BF16)
    head_ones = (jnp.arange(D_BRANCH)[:, None] // HEAD_DIM == jnp.arange(LANES)[None, :]).astype(BF16)
    sg_b_bc = jnp.broadcast_to(sg_b[..., None], sg_b.shape + (D_BRANCH // GMLP_GROUPS,))
    row = lambda a, l: a[l][None, :]
    for l in range(depth):
        mixed_ab, gate_c, qt, k, vt, gc, kmean, knorm = _proj_mix(
            x, w_in_b, sg_w[l], sg_b_bc[l], row(v_ln_g, l), row(v_ln_b, l), conv_w[l], row(conv_b, l),
            row(cv_ln_g, l), row(cv_ln_b, l), w_br_b, head_ones, layer=l, tm=tm)
        pad_blocks = lambda a: jnp.pad(a.reshape(B, nb, -1), ((0, 0), (0, MAX_BLOCKS - nb), (0, 0)))
        att = _moba(qt, k, vt, pad_blocks(kmean), pad_blocks(knorm), pos_const)
        x = _merge_out(x, mixed_ab, gate_c, att, gc, w_br_b, w_out_b, row(ln_g, l), row(ln_b, l),
                       layer=l, tm=tm, alpha=alpha)
    return x
```

```python
import functools

import numpy as np
import jax
import jax.numpy as jnp
from jax import lax
from jax.experimental import pallas as pl
from jax.experimental.pallas import tpu as pltpu

F32 = jnp.float32
BF16 = jnp.bfloat16

D_MODEL = 1024
D_BRANCH = 512
GMLP_GROUPS = 4
GMLP_CHUNK = 128
CONV_WIDTH = 31
CONV_HALO = 32
CONV_CHUNK = 128
ROW_CHUNK = 64
HEAD_DIM = 64
N_HEADS = 8
MOBA_BLOCK = 256
MOBA_TOPK = 3
N_BRANCH = 3
W_IN_COLS = 10 * D_BRANCH + N_BRANCH * D_MODEL
LN_EPS = 1e-5
MASK_VALUE = -1e30
LANES = 128
SUBLANES = 8
LOG2E = 1.4426950408889634
N_PAIRS = N_HEADS * HEAD_DIM // LANES
MAX_BLOCKS = 32
VMEM_LIMIT_BYTES = 56 * 1024 * 1024
FAR_SKIP_CLASSES = (((4,), 12), ((2, 3), 6), ((0, 1), 2))
FAR_SKIP_LOG2 = -100.0
FAR_SKIP_REL_SLACK = 0.02
FAR_SKIP_ABS_SLACK = 1.0

_C_AU, _C_AV, _C_AG = 0, 512, 1024
_C_BVAL, _C_BGLU, _C_BG = 1536, 2048, 2560
_C_Q, _C_K, _C_V, _C_CG = 3072, 3584, 4096, 4608
_C_MERGE = 5120

N_SPLIT = 3
_PC_ONE, _PC_BLK, _PC_POS = MAX_BLOCKS, MAX_BLOCKS + N_SPLIT, MAX_BLOCKS + 2 * N_SPLIT
_PC_ROWS = 16
_SUM_ROWS = 16


def _sigmoid(x):
    return 0.5 * jnp.tanh(0.5 * x) + 0.5


def _silu(x):
    return x * _sigmoid(x)


def _layer_norm(x, g, b):
    mu = jnp.mean(x, axis=-1, keepdims=True)
    xc = x - mu
    var = jnp.mean(xc * xc, axis=-1, keepdims=True)
    return xc * lax.rsqrt(var + LN_EPS) * g + b


def _dot(a, b):
    return jnp.dot(a, b, preferred_element_type=F32)


def _proj_mix_kernel(x_ref, w_in_ref, sg_w_ref, sg_b_ref, vg_ref, vb_ref, cw_ref, cb_ref,
                     cg_ref, cbeta_ref, wbr_ref, head_ones_ref,
                     mixed_ref, gate_c_ref, qt_ref, k_ref, vt_ref, gc_ref, kmean_ref, knorm_ref,
                     h_ref, xb_ref, conv_buf, shift_buf, vn_ref, y_ref, *, tm):
    i = pl.program_id(1)
    xb_ref[...] = x_ref[...].astype(BF16)
    chunks = [slice(r * ROW_CHUNK, (r + 1) * ROW_CHUNK) for r in range(tm // ROW_CHUNK)]
    cols = lambda c0, width=D_BRANCH: slice(c0, c0 + width)
    c_merge_a, c_merge_b, c_merge_c = _C_MERGE, _C_MERGE + D_MODEL, _C_MERGE + 2 * D_MODEL
    c_z = _C_AV
    c_br_a = _C_AU
    c_conv = _C_BVAL
    c_br_b = _C_AU

    def proj(c0, width=D_BRANCH):
        h_ref[:, cols(c0, width)] = _dot(xb_ref[...], w_in_ref[:, cols(c0, width)])

    proj(_C_AV)
    for rs in chunks:
        vn_ref[rs, :] = _layer_norm(h_ref[rs, cols(_C_AV)], vg_ref[...], vb_ref[...]).astype(BF16)
    proj(_C_AU)
    proj(_C_AG)
    proj(c_merge_a, D_MODEL)
    row = lax.broadcasted_iota(jnp.int32, (GMLP_CHUNK, GMLP_CHUNK), 0)
    col = lax.broadcasted_iota(jnp.int32, (GMLP_CHUNK, GMLP_CHUNK), 1)
    cgw = D_BRANCH // GMLP_GROUPS
    for g in range(GMLP_GROUPS):
        wm = jnp.where(col <= row, sg_w_ref[g], 0.0).astype(BF16)
        for c in range(tm // GMLP_CHUNK):
            rs = slice(c * GMLP_CHUNK, (c + 1) * GMLP_CHUNK)
            h_ref[rs, cols(c_z + g * cgw, cgw)] = _dot(wm, vn_ref[rs, cols(g * cgw, cgw)]) + sg_b_ref[g]
    for rs in chunks:
        y_a = h_ref[rs, cols(_C_AU)] * h_ref[rs, cols(c_z)] * _silu(h_ref[rs, cols(_C_AG)])
        y_ref[rs, :] = y_a.astype(BF16)

    proj(_C_BVAL)
    proj(_C_BGLU)

    @pl.when(i == 0)
    def _():
        conv_buf[0:CONV_HALO, :] = jnp.zeros((CONV_HALO, D_BRANCH), F32)

    for rs in chunks:
        glu = h_ref[rs, cols(_C_BVAL)] * _sigmoid(h_ref[rs, cols(_C_BGLU)])
        conv_buf[CONV_HALO + rs.start:CONV_HALO + rs.stop, :] = glu
    h_ref[:, cols(c_br_a, D_MODEL)] = _dot(y_ref[...], wbr_ref[0])
    rc = CONV_CHUNK
    first = CONV_HALO - (CONV_WIDTH - 1)

    def zero_after(c0):
        bits = pltpu.bitcast(h_ref[0:SUBLANES, cols(c0, LANES)], jnp.uint32)
        return pltpu.bitcast(lax.shift_right_logical(lax.shift_right_logical(bits, jnp.uint32(16)), jnp.uint32(16)),
                             F32)

    def conv_chunk(r, lc, after=None):
        ls = slice(lc * LANES, (lc + 1) * LANES)
        acc = jnp.zeros((rc, LANES), F32)
        if after is not None:
            acc = acc + jnp.concatenate([zero_after(after)] * (rc // SUBLANES), axis=0)
        for phase in range(SUBLANES):
            taps = [j for j in range(CONV_WIDTH) if (first + j) % SUBLANES == phase]
            if not taps:
                continue
            offs = [(first + j) // SUBLANES * SUBLANES for j in taps]
            r0 = r * rc + phase
            n = rc + max(offs)
            shift_buf[phase, 0:n, :] = conv_buf[r0:r0 + n, ls]
            window = shift_buf[phase, 0:n, :]
            for j, a in zip(taps, offs):
                acc = acc + window[a:a + rc, :] * cw_ref[j:j + 1, ls]
        h_ref[r * rc:(r + 1) * rc, cols(c_conv + lc * LANES, LANES)] = acc

    piece = 2 * LANES
    pieces_of = lambda sections: [c0 + off for c0, width in sections for off in range(0, width, piece)]
    fillers = pieces_of(((_C_BG, D_BRANCH), (c_merge_b, D_MODEL), (_C_CG, D_BRANCH)))
    n_regions = tm // rc
    n_lc = D_BRANCH // LANES
    every = (n_regions * n_lc) // len(fillers)
    for r in range(n_regions):
        @pl.when(i >= 0)
        def _(r=r):
            after = None
            for lc in range(n_lc):
                conv_chunk(r, lc, after)
                n = r * n_lc + lc
                if (n + 1) % every == 0:
                    after = fillers[n // every]
                    proj(after, piece)
    conv_buf[0:CONV_HALO, :] = conv_buf[tm:tm + CONV_HALO, :]

    late = pieces_of(((c_merge_c, D_MODEL), (_C_Q, D_BRANCH), (_C_K, D_BRANCH), (_C_V, D_BRANCH)))
    for rs in chunks:
        h_ref[rs, cols(c_merge_a, D_MODEL)] = (
            _sigmoid(h_ref[rs, cols(c_merge_a, D_MODEL)]) * h_ref[rs, cols(c_br_a, D_MODEL)])
        if late:
            proj(late.pop(0), piece)
    for rs in chunks:
        cv = _silu(_layer_norm(h_ref[rs, cols(c_conv)] + cb_ref[...], cg_ref[...], cbeta_ref[...]))
        y_ref[rs, :] = (cv * _silu(h_ref[rs, cols(_C_BG)])).astype(BF16)
        gc_ref[rs, :] = _silu(h_ref[rs, cols(_C_CG)]).astype(BF16)
        gate_c_ref[rs, :] = _sigmoid(h_ref[rs, cols(c_merge_c, D_MODEL)]).astype(BF16)
        if late:
            proj(late.pop(0), piece)
    for c0 in late:
        proj(c0, piece)

    h_ref[:, cols(c_br_b, D_MODEL)] = _dot(y_ref[...], wbr_ref[1])
    for rs in chunks:
        k_ref[rs, :] = h_ref[rs, cols(_C_K)].astype(BF16)
    for c in range(tm // MOBA_BLOCK):
        rs = slice(c * MOBA_BLOCK, (c + 1) * MOBA_BLOCK)
        qt_ref[:, rs] = (h_ref[rs, cols(_C_Q)] * (HEAD_DIM ** -0.5 * LOG2E)).T.astype(BF16)
        vt_ref[c] = h_ref[rs, cols(_C_V)].T.astype(BF16)
        kblk = h_ref[rs, cols(_C_K)]
        kmean_ref[c] = jnp.mean(kblk, axis=0, keepdims=True)
        norm2 = _dot((kblk * kblk).astype(BF16), head_ones_ref[...])
        knorm_ref[c] = jnp.max(norm2, axis=0, keepdims=True)
    for rs in chunks:
        gated_b = _sigmoid(h_ref[rs, cols(c_merge_b, D_MODEL)]) * h_ref[rs, cols(c_br_b, D_MODEL)]
        mixed_ref[rs, :] = (h_ref[rs, cols(c_merge_a, D_MODEL)] + gated_b).astype(BF16)


def _proj_mix(x, w_in, sg_w, sg_b_bc, v_ln_g, v_ln_b, conv_w, conv_b, cv_ln_g, cv_ln_b, w_branch, head_ones,
              *, layer, tm):
    B, S, D = x.shape
    nb = S // MOBA_BLOCK
    bpt = tm // MOBA_BLOCK
    grid = (B, S // tm)
    row_spec = lambda width: pl.BlockSpec((None, tm, width), lambda b, i: (b, i, 0))
    full = lambda shape: pl.BlockSpec(shape, lambda b, i: (0,) * len(shape))
    vec = full((1, D_BRANCH))
    return pl.pallas_call(
        functools.partial(_proj_mix_kernel, tm=tm),
        grid=grid,
        in_specs=[
            row_spec(D),
            pl.BlockSpec((None, D, W_IN_COLS), lambda b, i: (layer, 0, 0), pipeline_mode=pl.Buffered(1)),
            full((GMLP_GROUPS, GMLP_CHUNK, GMLP_CHUNK)),
            full((GMLP_GROUPS, GMLP_CHUNK, GMLP_CHUNK)),
            vec, vec,
            full((CONV_WIDTH, D_BRANCH)),
            vec, vec, vec,
            pl.BlockSpec((None, 2, D_BRANCH, D_MODEL), lambda b, i: (layer, 0, 0, 0)),
            full((D_BRANCH, LANES)),
        ],
        out_specs=[row_spec(D_MODEL), row_spec(D_MODEL),
                   pl.BlockSpec((None, D_BRANCH, tm), lambda b, i: (b, 0, i)),
                   row_spec(D_BRANCH),
                   pl.BlockSpec((None, bpt, D_BRANCH, MOBA_BLOCK), lambda b, i: (b, i, 0, 0)),
                   row_spec(D_BRANCH),
                   pl.BlockSpec((None, bpt, 1, D_BRANCH), lambda b, i: (b, i, 0, 0)),
                   pl.BlockSpec((None, bpt, 1, LANES), lambda b, i: (b, i, 0, 0))],
        out_shape=[
            jax.ShapeDtypeStruct((B, S, D_MODEL), BF16),
            jax.ShapeDtypeStruct((B, S, D_MODEL), BF16),
            jax.ShapeDtypeStruct((B, D_BRANCH, S), BF16),
            jax.ShapeDtypeStruct((B, S, D_BRANCH), BF16),
            jax.ShapeDtypeStruct((B, nb, D_BRANCH, MOBA_BLOCK), BF16),
            jax.ShapeDtypeStruct((B, S, D_BRANCH), BF16),
            jax.ShapeDtypeStruct((B, nb, 1, D_BRANCH), F32),
            jax.ShapeDtypeStruct((B, nb, 1, LANES), F32),
        ],
        scratch_shapes=[pltpu.VMEM((tm, W_IN_COLS), F32),
                        pltpu.VMEM((tm, D_MODEL), BF16),
                        pltpu.VMEM((CONV_HALO + tm, D_BRANCH), F32),
                        pltpu.VMEM((SUBLANES, CONV_CHUNK + CONV_HALO, LANES), F32),
                        pltpu.VMEM((tm, D_BRANCH), BF16),
                        pltpu.VMEM((tm, D_BRANCH), BF16)],
        compiler_params=pltpu.CompilerParams(
            dimension_semantics=("arbitrary", "arbitrary"), vmem_limit_bytes=VMEM_LIMIT_BYTES),
        name="proj_mix",
    )(x, w_in, sg_w, sg_b_bc, v_ln_g, v_ln_b, conv_w, conv_b, cv_ln_g, cv_ln_b, w_branch, head_ones)


def _split_bf16(c, parts=N_SPLIT):
    out, rest = [], np.float32(c)
    for _ in range(parts):
        piece = np.float32(rest.astype(BF16))
        out.append(float(piece))
        rest = np.float32(rest - piece)
    return out


def _moba_kernel(qt_ref, k_ref, vt_ref, pc_ref, km_ref, kn_ref, o_ref, qx_ref, s_ref, m_ref, acc_ref):
    qb = pl.program_id(1)
    blk = MOBA_BLOCK
    nbm = MAX_BLOCKS
    rown = lax.broadcasted_iota(jnp.int32, (nbm, blk), 0)
    row_f = lax.broadcasted_iota(jnp.int32, (LANES, blk), 0)
    head_rows = (row_f < HEAD_DIM, row_f >= HEAD_DIM)
    qbv = jnp.full((nbm, blk), qb, jnp.int32)
    rowp = lax.broadcasted_iota(jnp.int32, (_PC_ROWS, blk), 0) + nbm
    t_pos = (qbv[0:_PC_ROWS, :] * blk + lax.broadcasted_iota(jnp.int32, (_PC_ROWS, blk), 1)).astype(F32)
    neg_inf = jnp.float32(-jnp.inf)
    zeros_tail = jnp.zeros((LANES - nbm - _PC_ROWS, blk), F32)
    ones_rows = jnp.ones((_SUM_ROWS, blk), BF16)

    for p in range(N_PAIRS):
        qtp = qt_ref[p * LANES:(p + 1) * LANES, :]
        km = km_ref[:, p * LANES:(p + 1) * LANES]
        km_hi = km.astype(BF16)
        km_lo = (km - km_hi.astype(F32)).astype(BF16)
        for h in range(2):
            c_parts = _split_bf16(2.0 ** (-(2 * p + h + 1) * 8.0 / N_HEADS) * LOG2E)
            qh = jnp.where(head_rows[h], qtp, jnp.zeros_like(qtp))
            gate = _dot(km_hi, qh) + _dot(km_lo, qh)
            g = jnp.where(rown < qbv, gate, neg_inf)
            selected = rown < 0
            for r in range(MOBA_TOPK):
                mx = jnp.max(g, axis=0, keepdims=True)
                idx = jnp.min(jnp.where(g == mx, rown, nbm), axis=0, keepdims=True)
                pick = rown == idx
                selected = jnp.logical_or(selected, jnp.logical_and(pick, qbv > r))
                g = jnp.where(pick, neg_inf, g)
            mask_rows = jnp.where(selected, 0.0, MASK_VALUE)
            bias_t = -sum(c_parts) * t_pos
            bias_t1 = bias_t.astype(BF16).astype(F32)
            bias_t2 = (bias_t - bias_t1).astype(BF16).astype(F32)
            pos_rows = jnp.zeros((_PC_ROWS, blk), F32)
            for i, piece in enumerate((bias_t1, bias_t2, bias_t - bias_t1 - bias_t2)):
                pos_rows = jnp.where(rowp == _PC_ONE + i, piece, pos_rows)
            for i, c_i in enumerate(c_parts):
                pos_rows = jnp.where(jnp.logical_or(rowp == _PC_BLK + i, rowp == _PC_POS + i), c_i, pos_rows)
            extra = jnp.concatenate([mask_rows, pos_rows, zeros_tail], axis=0)
            qx_ref[2 * p + h] = jnp.concatenate([qh, extra.astype(BF16)], axis=0)

    def values(vt_blk, hd):
        return jnp.concatenate([vt_blk[hd * HEAD_DIM:(hd + 1) * HEAD_DIM, :], ones_rows], axis=0)

    lane_pc = lax.broadcasted_iota(jnp.int32, (blk, LANES), 1)
    n_slots = qb // 2 * 2

    def block_rows(j):
        return pl.ds(j * blk if isinstance(j, int) else pl.multiple_of(j * blk, blk), blk)

    def pos_tile(j, final):
        pc_j = pc_ref[block_rows(j), :]
        drop =jnp.logical_and(jnp.full((blk, LANES), final, jnp.int32) != 0, lane_pc < nbm)
        return jnp.where(drop, jnp.zeros_like(pc_j), pc_j)

    def update(hd, s, vt_blk):
        m_old = m_ref[hd]
        m_new = jnp.maximum(m_old, jnp.max(s, axis=0, keepdims=True))
        alpha = jnp.exp2(m_old - m_new)
        pexp = jnp.exp2(s - m_new)
        acc_ref[hd] = alpha * acc_ref[hd] + _dot(values(vt_blk, hd), pexp.astype(BF16))
        m_ref[hd] = m_new

    def scores(dst, hd, j, pc_j):
        kk = jnp.concatenate([k_ref[block_rows(j), (hd // 2) * LANES:(hd // 2 + 1) * LANES], pc_j], axis=1)
        s_ref[dst, hd] = _dot(kk, qx_ref[hd])

    def phase(src, dst, vt_blk, j_next, final_next, heads):
        lead = 2
        pc_n = pos_tile(j_next, final_next)
        for step in range(len(heads) + lead):
            if step < len(heads):
                scores(dst, heads[step], j_next, pc_n)
            if step >= lead:
                update(heads[step - lead], s_ref[src, heads[step - lead]], vt_blk)

    def run_slots(t0, count, heads):
        for u in range(count):
            t_next = t0 + u + 1
            final_next = (t_next == qb).astype(jnp.int32)
            phase(u % 2, 1 - u % 2, vt_ref[t0 + u], t_next, final_next, heads)

    unroll = 4
    key_dist = ((qbv - rown) * blk + lax.broadcasted_iota(jnp.int32, (nbm, blk), 1) - (blk - 1)).astype(F32)
    kt_own = {}
    join_at = []
    for heads, window in FAR_SKIP_CLASSES:
        t_max = jnp.maximum(qb - window, 0) // unroll * unroll
        class_ok = None
        for hd in heads:
            p, h = divmod(hd, 2)
            if p not in kt_own:
                kt_own[p] = k_ref[block_rows(qb), p * LANES:(p + 1) * LANES].astype(F32).T
            c_hd = 2.0 ** (-(hd + 1) * 8.0 / N_HEADS) * LOG2E
            qf = qt_ref[hd * HEAD_DIM:(hd + 1) * HEAD_DIM, :].astype(F32)
            q_norm = jnp.sqrt(jnp.sum(qf * qf, axis=0, keepdims=True))
            diag = jnp.sum(qf * kt_own[p][h * HEAD_DIM:(h + 1) * HEAD_DIM, :], axis=0, keepdims=True)
            k_norm = jnp.sqrt(kn_ref[:, hd:hd + 1])
            slack = FAR_SKIP_REL_SLACK * q_norm * jnp.max(k_norm, axis=0, keepdims=True) + FAR_SKIP_ABS_SLACK
            bound = (1.0 + FAR_SKIP_REL_SLACK) * k_norm * q_norm - c_hd * key_dist - diag + slack
            masked = jnp.where(rown < t_max, bound, neg_inf)
            worst = jnp.max(jnp.max(masked, axis=0, keepdims=True), axis=1, keepdims=True)[0, 0]
            ok = worst < FAR_SKIP_LOG2
            class_ok = ok if class_ok is None else jnp.logical_and(class_ok, ok)
        join_at.append(jnp.where(class_ok, t_max, 0))
    for c in range(len(join_at) - 2, -1, -1):
        join_at[c] = jnp.minimum(join_at[c], join_at[c + 1])

    m_ref[...] = jnp.full(m_ref.shape, -jnp.inf, F32)
    acc_ref[...] = jnp.zeros(acc_ref.shape, F32)
    first_final = (qb == 0).astype(jnp.int32)
    active = tuple(hd for hd in range(N_HEADS) if all(hd not in heads for heads, _ in FAR_SKIP_CLASSES))
    pc_0 = pos_tile(0, first_final)
    for hd in active:
        scores(0, hd, 0, pc_0)

    def slot_loop(t_from, t_to, heads):
        def trip(i, carry):
            run_slots(t_from + unroll * i, unroll, heads)
            return carry
        lax.fori_loop(0, (t_to - t_from) // unroll, trip, 0)

    t_prev = 0
    for (heads, _), t_join in zip(FAR_SKIP_CLASSES, join_at):
        slot_loop(t_prev, t_join, active)
        pc_join = pos_tile(jnp.minimum(t_join, qb), jnp.where(t_join == 0, first_final, 0))
        for hd in heads:
            scores(0, hd, jnp.minimum(t_join, qb), pc_join)
        active, t_prev = active + heads, t_join
    slot_loop(t_prev, n_slots, active)

    @pl.when((n_slots - t_prev) % unroll != 0)
    def _():
        run_slots(n_slots - 2, 2, active)

    def finish(buf):
        key_i = lax.broadcasted_iota(jnp.int32, (blk, blk), 0)
        qry_i = lax.broadcasted_iota(jnp.int32, (blk, blk), 1)
        vt_own = vt_ref[qb]
        for hd in range(N_HEADS):
            update(hd, jnp.where(key_i <= qry_i, s_ref[buf, hd], MASK_VALUE), vt_own)
        outs = []
        for hd in range(N_HEADS):
            acc = acc_ref[hd]
            outs.append(acc[0:HEAD_DIM, :] / acc[HEAD_DIM:HEAD_DIM + 1, :])
        o_ref[...] = jnp.concatenate(outs, axis=0).T.astype(o_ref.dtype)

    @pl.when(qb % 2 == 1)
    def _():
        phase(0, 1, vt_ref[qb - 1], qb, 1, active)
        finish(1)

    @pl.when(qb % 2 == 0)
    def _():
        finish(0)


def _moba(qt, k, vt, kmean, knorm, pos_const):
    B, S, _ = k.shape
    nb = S // MOBA_BLOCK
    return pl.pallas_call(
        _moba_kernel,
        grid=(B, nb),
        in_specs=[
            pl.BlockSpec((None, D_BRANCH, MOBA_BLOCK), lambda b, i: (b, 0, i)),
            pl.BlockSpec((None, S, D_BRANCH), lambda b, i: (b, 0, 0)),
            pl.BlockSpec((None, nb, D_BRANCH, MOBA_BLOCK), lambda b, i: (b, 0, 0, 0)),
            pl.BlockSpec((S, LANES), lambda b, i: (0, 0)),
            pl.BlockSpec((None, MAX_BLOCKS, D_BRANCH), lambda b, i: (b, 0, 0)),
            pl.BlockSpec((None, MAX_BLOCKS, LANES), lambda b, i: (b, 0, 0)),
        ],
        out_specs=pl.BlockSpec((None, MOBA_BLOCK, D_BRANCH), lambda b, i: (b, i, 0)),
        out_shape=jax.ShapeDtypeStruct((B, S, D_BRANCH), BF16),
        scratch_shapes=[
            pltpu.VMEM((N_HEADS, 2 * LANES, MOBA_BLOCK), BF16),
            pltpu.VMEM((2, N_HEADS, MOBA_BLOCK, MOBA_BLOCK), F32),
            pltpu.VMEM((N_HEADS, 1, MOBA_BLOCK), F32),
            pltpu.VMEM((N_HEADS, HEAD_DIM + _SUM_ROWS, MOBA_BLOCK), F32),
        ],
        compiler_params=pltpu.CompilerParams(
            dimension_semantics=("arbitrary", "arbitrary"), vmem_limit_bytes=VMEM_LIMIT_BYTES),
        name="moba",
    )(qt, k, vt, pos_const, kmean, knorm)


def _merge_out_kernel(x_ref, mixed_ref, gate_c_ref, att_ref, gc_ref, wbc_ref, wout_ref, g_ref, b_ref,
                      o_ref, *, alpha):
    y_c = (att_ref[...].astype(F32) * gc_ref[...].astype(F32)).astype(BF16)
    mixed = mixed_ref[...].astype(F32) + gate_c_ref[...].astype(F32) * _dot(y_c, wbc_ref[...])
    out = _dot(mixed.astype(BF16), wout_ref[...])
    o_ref[...] = _layer_norm(alpha * x_ref[...] + out, g_ref[...], b_ref[...])


def _merge_out(x, mixed_ab, gate_c, att, gc, w_branch, w_out, ln_g, ln_b, *, layer, tm, alpha):
    B, S, D = x.shape
    row_spec = lambda width: pl.BlockSpec((None, tm, width), lambda b, i: (b, i, 0))
    full = lambda shape: pl.BlockSpec(shape, lambda b, i: (0,) * len(shape))
    return pl.pallas_call(
        functools.partial(_merge_out_kernel, alpha=alpha),
        grid=(B, S // tm),
        in_specs=[row_spec(D), row_spec(D_MODEL), row_spec(D_MODEL), row_spec(D_BRANCH), row_spec(D_BRANCH),
                  pl.BlockSpec((None, None, D_BRANCH, D_MODEL), lambda b, i: (layer, 2, 0, 0)),
                  pl.BlockSpec((None, D_MODEL, D_MODEL), lambda b, i: (layer, 0, 0)),
                  full((1, D_MODEL)), full((1, D_MODEL))],
        out_specs=row_spec(D_MODEL),
        out_shape=jax.ShapeDtypeStruct((B, S, D_MODEL), F32),
        compiler_params=pltpu.CompilerParams(
            dimension_semantics=("arbitrary", "arbitrary"), vmem_limit_bytes=VMEM_LIMIT_BYTES),
        name="merge_out",
    )(x, mixed_ab, gate_c, att, gc, w_branch, w_out, ln_g, ln_b)


def _position_constants(S):
    s = jnp.arange(S, dtype=jnp.int32)
    blk, pos = s // MOBA_BLOCK, s % MOBA_BLOCK
    lane = jnp.arange(LANES, dtype=jnp.int32)[None, :]
    pc = (lane == blk[:, None]).astype(F32)
    pc = jnp.where((lane >= _PC_ONE) & (lane < _PC_ONE + N_SPLIT), 1.0, pc)
    pc = jnp.where((lane >= _PC_BLK) & (lane < _PC_BLK + N_SPLIT), (blk * MOBA_BLOCK).astype(F32)[:, None], pc)
    pc = jnp.where((lane >= _PC_POS) & (lane < _PC_POS + N_SPLIT), pos.astype(F32)[:, None], pc)
    return pc.astype(BF16)


def kernel(x, w_in, sg_w, sg_b, v_ln_g, v_ln_b, conv_w, conv_b, cv_ln_g, cv_ln_b, w_branch, w_out, ln_g, ln_b):
    depth = w_in.shape[0]
    B, S, D = x.shape
    nb = S // MOBA_BLOCK
    assert D == D_MODEL and S % MOBA_BLOCK == 0 and nb <= MAX_BLOCKS == _PC_ONE
    alpha = (2 * depth) ** 0.25
    tm = 512
    tm_out = min(1024, S)
    pos_const = _position_constants(S)
    w_in_b = w_in.astype(BF16)
    w_br_b = w_branch.astype(BF16)
    w_out_b = w_out.astype(BF16)
    head_ones = (jnp.arange(D_BRANCH)[:, None] // HEAD_DIM == jnp.arange(LANES)[None, :]).astype(BF16)
    sg_b_bc = jnp.broadcast_to(sg_b[..., None], sg_b.shape + (D_BRANCH // GMLP_GROUPS,))
    row = lambda a, l: a[l][None, :]
    for l in range(depth):
        mixed_ab, gate_c, qt, k, vt, gc, kmean, knorm = _proj_mix(
            x, w_in_b, sg_w[l], sg_b_bc[l], row(v_ln_g, l), row(v_ln_b, l), conv_w[l], row(conv_b, l),
            row(cv_ln_g, l), row(cv_ln_b, l), w_br_b, head_ones, layer=l, tm=tm)
        pad_blocks = lambda a: jnp.pad(a.reshape(B, nb, -1), ((0, 0), (0, MAX_BLOCKS - nb), (0, 0)))
        att = _moba(qt, k, vt, pad_blocks(kmean), pad_blocks(knorm), pos_const)
        x = _merge_out(x, mixed_ab, gate_c, att, gc, w_br_b, w_out_b, row(ln_g, l), row(ln_b, l),
                       layer=l, tm=tm_out, alpha=alpha)
    return x
```

```python
import functools

import numpy as np
import jax
import jax.numpy as jnp
from jax import lax
from jax.experimental import pallas as pl
from jax.experimental.pallas import tpu as pltpu

F32 = jnp.float32
BF16 = jnp.bfloat16

D_MODEL = 1024
D_BRANCH = 512
GMLP_GROUPS = 4
GMLP_CHUNK = 128
CONV_WIDTH = 31
CONV_HALO = 32
CONV_CHUNK = 128
ROW_CHUNK = 64
HEAD_DIM = 64
N_HEADS = 8
MOBA_BLOCK = 256
MOBA_TOPK = 3
N_BRANCH = 3
W_IN_COLS = 10 * D_BRANCH + N_BRANCH * D_MODEL
LN_EPS = 1e-5
MASK_VALUE = -1e30
LANES = 128
SUBLANES = 8
LOG2E = 1.4426950408889634
N_PAIRS = N_HEADS * HEAD_DIM // LANES
MAX_BLOCKS = 32
VMEM_LIMIT_BYTES = 56 * 1024 * 1024
FAR_SKIP_CLASSES = (((4,), 12), ((2, 3), 6), ((0, 1), 2))
FAR_SKIP_LOG2 = -100.0
FAR_SKIP_REL_SLACK = 0.02
FAR_SKIP_ABS_SLACK = 1.0

_C_AU, _C_AV, _C_AG = 0, 512, 1024
_C_BVAL, _C_BGLU, _C_BG = 1536, 2048, 2560
_C_Q, _C_K, _C_V, _C_CG = 3072, 3584, 4096, 4608
_C_MERGE = 5120

N_SPLIT = 3
_PC_ONE, _PC_BLK, _PC_POS = MAX_BLOCKS, MAX_BLOCKS + N_SPLIT, MAX_BLOCKS + 2 * N_SPLIT
_PC_ROWS = 16
_SUM_ROWS = 16


def _sigmoid(x):
    return 0.5 * jnp.tanh(0.5 * x) + 0.5


def _silu(x):
    return x * _sigmoid(x)


def _layer_norm(x, g, b):
    mu = jnp.mean(x, axis=-1, keepdims=True)
    xc = x - mu
    var = jnp.mean(xc * xc, axis=-1, keepdims=True)
    return xc * lax.rsqrt(var + LN_EPS) * g + b


def _dot(a, b):
    return jnp.dot(a, b, preferred_element_type=F32)


def _proj_mix_kernel(x_ref, w_in_ref, sg_w_ref, sg_b_ref, vg_ref, vb_ref, cw_ref, cb_ref,
                     cg_ref, cbeta_ref, wbr_ref, head_ones_ref,
                     mixed_ref, gate_c_ref, qt_ref, k_ref, vt_ref, gc_ref, kmean_ref, knorm_ref,
                     h_ref, xb_ref, conv_buf, shift_buf, vn_ref, y_ref, *, tm):
    i = pl.program_id(1)
    xb_ref[...] = x_ref[...].astype(BF16)
    chunks = [slice(r * ROW_CHUNK, (r + 1) * ROW_CHUNK) for r in range(tm // ROW_CHUNK)]
    cols = lambda c0, width=D_BRANCH: slice(c0, c0 + width)
    c_merge_a, c_merge_b, c_merge_c = _C_MERGE, _C_MERGE + D_MODEL, _C_MERGE + 2 * D_MODEL
    c_z = _C_AV
    c_br_a = _C_AU
    c_conv = _C_BVAL
    c_br_b = _C_AU

    def proj(c0, width=D_BRANCH):
        h_ref[:, cols(c0, width)] = _dot(xb_ref[...], w_in_ref[:, cols(c0, width)])

    proj(_C_AV)
    for rs in chunks:
        vn_ref[rs, :] = _layer_norm(h_ref[rs, cols(_C_AV)], vg_ref[...], vb_ref[...]).astype(BF16)
    proj(_C_AU)
    proj(_C_AG)
    proj(c_merge_a, D_MODEL)
    row = lax.broadcasted_iota(jnp.int32, (GMLP_CHUNK, GMLP_CHUNK), 0)
    col = lax.broadcasted_iota(jnp.int32, (GMLP_CHUNK, GMLP_CHUNK), 1)
    cgw = D_BRANCH // GMLP_GROUPS
    for g in range(GMLP_GROUPS):
        wm = jnp.where(col <= row, sg_w_ref[g], 0.0).astype(BF16)
        for c in range(tm // GMLP_CHUNK):
            rs = slice(c * GMLP_CHUNK, (c + 1) * GMLP_CHUNK)
            h_ref[rs, cols(c_z + g * cgw, cgw)] = _dot(wm, vn_ref[rs, cols(g * cgw, cgw)]) + sg_b_ref[g]
    for rs in chunks:
        y_a = h_ref[rs, cols(_C_AU)] * h_ref[rs, cols(c_z)] * _silu(h_ref[rs, cols(_C_AG)])
        y_ref[rs, :] = y_a.astype(BF16)

    proj(_C_BVAL)
    proj(_C_BGLU)

    @pl.when(i == 0)
    def _():
        conv_buf[0:CONV_HALO, :] = jnp.zeros((CONV_HALO, D_BRANCH), F32)

    for rs in chunks:
        glu = h_ref[rs, cols(_C_BVAL)] * _sigmoid(h_ref[rs, cols(_C_BGLU)])
        conv_buf[CONV_HALO + rs.start:CONV_HALO + rs.stop, :] = glu
    h_ref[:, cols(c_br_a, D_MODEL)] = _dot(y_ref[...], wbr_ref[0])
    rc = CONV_CHUNK
    first = CONV_HALO - (CONV_WIDTH - 1)

    def zero_after(c0):
        bits = pltpu.bitcast(h_ref[0:SUBLANES, cols(c0, LANES)], jnp.uint32)
        return pltpu.bitcast(lax.shift_right_logical(lax.shift_right_logical(bits, jnp.uint32(16)), jnp.uint32(16)),
                             F32)

    def conv_chunk(r, lc, after=None):
        ls = slice(lc * LANES, (lc + 1) * LANES)
        acc = jnp.zeros((rc, LANES), F32)
        if after is not None:
            acc = acc + jnp.concatenate([zero_after(after)] * (rc // SUBLANES), axis=0)
        for phase in range(SUBLANES):
            taps = [j for j in range(CONV_WIDTH) if (first + j) % SUBLANES == phase]
            if not taps:
                continue
            offs = [(first + j) // SUBLANES * SUBLANES for j in taps]
            r0 = r * rc + phase
            n = rc + max(offs)
            shift_buf[phase, 0:n, :] = conv_buf[r0:r0 + n, ls]
            window = shift_buf[phase, 0:n, :]
            for j, a in zip(taps, offs):
                acc = acc + window[a:a + rc, :] * cw_ref[j:j + 1, ls]
        h_ref[r * rc:(r + 1) * rc, cols(c_conv + lc * LANES, LANES)] = acc

    piece = 2 * LANES
    pieces_of = lambda sections: [c0 + off for c0, width in sections for off in range(0, width, piece)]
    fillers = pieces_of(((_C_BG, D_BRANCH), (c_merge_b, D_MODEL), (_C_CG, D_BRANCH)))
    n_regions = tm // rc
    n_lc = D_BRANCH // LANES
    every = (n_regions * n_lc) // len(fillers)
    for r in range(n_regions):
        @pl.when(i >= 0)
        def _(r=r):
            after = None
            for lc in range(n_lc):
                conv_chunk(r, lc, after)
                n = r * n_lc + lc
                if (n + 1) % every == 0:
                    after = fillers[n // every]
                    proj(after, piece)
    conv_buf[0:CONV_HALO, :] = conv_buf[tm:tm + CONV_HALO, :]

    late = pieces_of(((c_merge_c, D_MODEL), (_C_Q, D_BRANCH), (_C_K, D_BRANCH), (_C_V, D_BRANCH)))
    for rs in chunks:
        h_ref[rs, cols(c_merge_a, D_MODEL)] = (
            _sigmoid(h_ref[rs, cols(c_merge_a, D_MODEL)]) * h_ref[rs, cols(c_br_a, D_MODEL)])
        if late:
            proj(late.pop(0), piece)
    for rs in chunks:
        cv = _silu(_layer_norm(h_ref[rs, cols(c_conv)] + cb_ref[...], cg_ref[...], cbeta_ref[...]))
        y_ref[rs, :] = (cv * _silu(h_ref[rs, cols(_C_BG)])).astype(BF16)
        gc_ref[rs, :] = _silu(h_ref[rs, cols(_C_CG)]).astype(BF16)
        gate_c_ref[rs, :] = _sigmoid(h_ref[rs, cols(c_merge_c, D_MODEL)]).astype(BF16)
        if late:
            proj(late.pop(0), piece)
    for c0 in late:
        proj(c0, piece)

    h_ref[:, cols(c_br_b, D_MODEL)] = _dot(y_ref[...], wbr_ref[1])
    for rs in chunks:
        k_ref[rs, :] = h_ref[rs, cols(_C_K)].astype(BF16)
    for c in range(tm // MOBA_BLOCK):
        rs = slice(c * MOBA_BLOCK, (c + 1) * MOBA_BLOCK)
        qt_ref[:, rs] = (h_ref[rs, cols(_C_Q)] * (HEAD_DIM ** -0.5 * LOG2E)).T.astype(BF16)
        vt_ref[c] = h_ref[rs, cols(_C_V)].T.astype(BF16)
        kblk = h_ref[rs, cols(_C_K)]
        kmean_ref[c] = jnp.mean(kblk, axis=0, keepdims=True)
        norm2 = _dot((kblk * kblk).astype(BF16), head_ones_ref[...])
        knorm_ref[c] = jnp.max(norm2, axis=0, keepdims=True)
    for rs in chunks:
        gated_b = _sigmoid(h_ref[rs, cols(c_merge_b, D_MODEL)]) * h_ref[rs, cols(c_br_b, D_MODEL)]
        mixed_ref[rs, :] = (h_ref[rs, cols(c_merge_a, D_MODEL)] + gated_b).astype(BF16)


def _proj_mix(x, w_in, sg_w, sg_b_bc, v_ln_g, v_ln_b, conv_w, conv_b, cv_ln_g, cv_ln_b, w_branch, head_ones,
              *, layer, tm):
    B, S, D = x.shape
    nb = S // MOBA_BLOCK
    bpt = tm // MOBA_BLOCK
    grid = (B, S // tm)
    row_spec = lambda width: pl.BlockSpec((None, tm, width), lambda b, i: (b, i, 0))
    full = lambda shape: pl.BlockSpec(shape, lambda b, i: (0,) * len(shape))
    vec = full((1, D_BRANCH))
    return pl.pallas_call(
        functools.partial(_proj_mix_kernel, tm=tm),
        grid=grid,
        in_specs=[
            row_spec(D),
            pl.BlockSpec((None, D, W_IN_COLS), lambda b, i: (layer, 0, 0), pipeline_mode=pl.Buffered(1)),
            full((GMLP_GROUPS, GMLP_CHUNK, GMLP_CHUNK)),
            full((GMLP_GROUPS, GMLP_CHUNK, GMLP_CHUNK)),
            vec, vec,
            full((CONV_WIDTH, D_BRANCH)),
            vec, vec, vec,
            pl.BlockSpec((None, 2, D_BRANCH, D_MODEL), lambda b, i: (layer, 0, 0, 0)),
            full((D_BRANCH, LANES)),
        ],
        out_specs=[row_spec(D_MODEL), row_spec(D_MODEL),
                   pl.BlockSpec((None, D_BRANCH, tm), lambda b, i: (b, 0, i)),
                   row_spec(D_BRANCH),
                   pl.BlockSpec((None, bpt, D_BRANCH, MOBA_BLOCK), lambda b, i: (b, i, 0, 0)),
                   row_spec(D_BRANCH),
                   pl.BlockSpec((None, bpt, 1, D_BRANCH), lambda b, i: (b, i, 0, 0)),
                   pl.BlockSpec((None, bpt, 1, LANES), lambda b, i: (b, i, 0, 0))],
        out_shape=[
            jax.ShapeDtypeStruct((B, S, D_MODEL), BF16),
            jax.ShapeDtypeStruct((B, S, D_MODEL), BF16),
            jax.ShapeDtypeStruct((B, D_BRANCH, S), BF16),
            jax.ShapeDtypeStruct((B, S, D_BRANCH), BF16),
            jax.ShapeDtypeStruct((B, nb, D_BRANCH, MOBA_BLOCK), BF16),
            jax.ShapeDtypeStruct((B, S, D_BRANCH), BF16),
            jax.ShapeDtypeStruct((B, nb, 1, D_BRANCH), F32),
            jax.ShapeDtypeStruct((B, nb, 1, LANES), F32),
        ],
        scratch_shapes=[pltpu.VMEM((tm, W_IN_COLS), F32),
                        pltpu.VMEM((tm, D_MODEL), BF16),
                        pltpu.VMEM((CONV_HALO + tm, D_BRANCH), F32),
                        pltpu.VMEM((SUBLANES, CONV_CHUNK + CONV_HALO, LANES), F32),
                        pltpu.VMEM((tm, D_BRANCH), BF16),
                        pltpu.VMEM((tm, D_BRANCH), BF16)],
        compiler_params=pltpu.CompilerParams(
            dimension_semantics=("arbitrary", "arbitrary"), vmem_limit_bytes=VMEM_LIMIT_BYTES),
        name="proj_mix",
    )(x, w_in, sg_w, sg_b_bc, v_ln_g, v_ln_b, conv_w, conv_b, cv_ln_g, cv_ln_b, w_branch, head_ones)


def _split_bf16(c, parts=N_SPLIT):
    out, rest = [], np.float32(c)
    for _ in range(parts):
        piece = np.float32(rest.astype(BF16))
        out.append(float(piece))
        rest = np.float32(rest - piece)
    return out


def _moba_kernel(qt_ref, k_ref, vt_ref, pc_ref, km_ref, kn_ref, o_ref, qx_ref, s_ref, m_ref, acc_ref):
    qb = pl.program_id(1)
    blk = MOBA_BLOCK
    nbm = MAX_BLOCKS
    rown = lax.broadcasted_iota(jnp.int32, (nbm, blk), 0)
    row_f = lax.broadcasted_iota(jnp.int32, (LANES, blk), 0)
    head_rows = (row_f < HEAD_DIM, row_f >= HEAD_DIM)
    qbv = jnp.full((nbm, blk), qb, jnp.int32)
    rowp = lax.broadcasted_iota(jnp.int32, (_PC_ROWS, blk), 0) + nbm
    t_pos = (qbv[0:_PC_ROWS, :] * blk + lax.broadcasted_iota(jnp.int32, (_PC_ROWS, blk), 1)).astype(F32)
    neg_inf = jnp.float32(-jnp.inf)
    zeros_tail = jnp.zeros((LANES - nbm - _PC_ROWS, blk), F32)
    ones_rows = jnp.ones((_SUM_ROWS, blk), BF16)

    for p in range(N_PAIRS):
        qtp = qt_ref[p * LANES:(p + 1) * LANES, :]
        km = km_ref[:, p * LANES:(p + 1) * LANES]
        km_hi = km.astype(BF16)
        km_lo = (km - km_hi.astype(F32)).astype(BF16)
        for h in range(2):
            c_parts = _split_bf16(2.0 ** (-(2 * p + h + 1) * 8.0 / N_HEADS) * LOG2E)
            qh = jnp.where(head_rows[h], qtp, jnp.zeros_like(qtp))
            gate = _dot(km_hi, qh) + _dot(km_lo, qh)
            g = jnp.where(rown < qbv, gate, neg_inf)
            selected = rown < 0
            for r in range(MOBA_TOPK):
                mx = jnp.max(g, axis=0, keepdims=True)
                idx = jnp.min(jnp.where(g == mx, rown, nbm), axis=0, keepdims=True)
                pick = rown == idx
                selected = jnp.logical_or(selected, jnp.logical_and(pick, qbv > r))
                g = jnp.where(pick, neg_inf, g)
            mask_rows = jnp.where(selected, 0.0, MASK_VALUE)
            bias_t = -sum(c_parts) * t_pos
            bias_t1 = bias_t.astype(BF16).astype(F32)
            bias_t2 = (bias_t - bias_t1).astype(BF16).astype(F32)
            pos_rows = jnp.zeros((_PC_ROWS, blk), F32)
            for i, piece in enumerate((bias_t1, bias_t2, bias_t - bias_t1 - bias_t2)):
                pos_rows = jnp.where(rowp == _PC_ONE + i, piece, pos_rows)
            for i, c_i in enumerate(c_parts):
                pos_rows = jnp.where(jnp.logical_or(rowp == _PC_BLK + i, rowp == _PC_POS + i), c_i, pos_rows)
            extra = jnp.concatenate([mask_rows, pos_rows, zeros_tail], axis=0)
            qx_ref[2 * p + h] = jnp.concatenate([qh, extra.astype(BF16)], axis=0)

    def values(vt_blk, hd):
        return jnp.concatenate([vt_blk[hd * HEAD_DIM:(hd + 1) * HEAD_DIM, :], ones_rows], axis=0)

    lane_pc = lax.broadcasted_iota(jnp.int32, (blk, LANES), 1)
    n_slots = qb // 2 * 2

    def block_rows(j):
        return pl.ds(j * blk if isinstance(j, int) else pl.multiple_of(j * blk, blk), blk)

    def pos_tile(j, final):
        pc_j = pc_ref[block_rows(j), :]
        drop =jnp.logical_and(jnp.full((blk, LANES), final, jnp.int32) != 0, lane_pc < nbm)
        return jnp.where(drop, jnp.zeros_like(pc_j), pc_j)

    def update(hd, s, vt_blk):
        m_old = m_ref[hd]
        m_new = jnp.maximum(m_old, jnp.max(s, axis=0, keepdims=True))
        alpha = jnp.exp2(m_old - m_new)
        pexp = jnp.exp2(s - m_new)
        acc_ref[hd] = alpha * acc_ref[hd] + _dot(values(vt_blk, hd), pexp.astype(BF16))
        m_ref[hd] = m_new

    def scores(dst, hd, j, pc_j):
        kk = jnp.concatenate([k_ref[block_rows(j), (hd // 2) * LANES:(hd // 2 + 1) * LANES], pc_j], axis=1)
        s_ref[dst, hd] = _dot(kk, qx_ref[hd])

    def phase(src, dst, vt_blk, j_next, final_next, heads):
        lead = 2
        pc_n = pos_tile(j_next, final_next)
        for step in range(len(heads) + lead):
            if step < len(heads):
                scores(dst, heads[step], j_next, pc_n)
            if step >= lead:
                update(heads[step - lead], s_ref[src, heads[step - lead]], vt_blk)

    def run_slots(t0, count, heads):
        for u in range(count):
            t_next = t0 + u + 1
            final_next = (t_next == qb).astype(jnp.int32)
            phase(u % 2, 1 - u % 2, vt_ref[t0 + u], t_next, final_next, heads)

    unroll = 4
    key_dist = ((qbv - rown) * blk + lax.broadcasted_iota(jnp.int32, (nbm, blk), 1) - (blk - 1)).astype(F32)
    kt_own = {}
    join_at = []
    for heads, window in FAR_SKIP_CLASSES:
        t_max = jnp.maximum(qb - window, 0) // unroll * unroll
        class_ok = None
        for hd in heads:
            p, h = divmod(hd, 2)
            if p not in kt_own:
                kt_own[p] = k_ref[block_rows(qb), p * LANES:(p + 1) * LANES].astype(F32).T
            c_hd = 2.0 ** (-(hd + 1) * 8.0 / N_HEADS) * LOG2E
            qf = qt_ref[hd * HEAD_DIM:(hd + 1) * HEAD_DIM, :].astype(F32)
            q_norm = jnp.sqrt(jnp.sum(qf * qf, axis=0, keepdims=True))
            diag = jnp.sum(qf * kt_own[p][h * HEAD_DIM:(h + 1) * HEAD_DIM, :], axis=0, keepdims=True)
            k_norm = jnp.sqrt(kn_ref[:, hd:hd + 1])
            slack = FAR_SKIP_REL_SLACK * q_norm * jnp.max(k_norm, axis=0, keepdims=True) + FAR_SKIP_ABS_SLACK
            bound = (1.0 + FAR_SKIP_REL_SLACK) * k_norm * q_norm - c_hd * key_dist - diag + slack
            masked = jnp.where(rown < t_max, bound, neg_inf)
            worst = jnp.max(jnp.max(masked, axis=0, keepdims=True), axis=1, keepdims=True)[0, 0]
            ok = worst < FAR_SKIP_LOG2
            class_ok = ok if class_ok is None else jnp.logical_and(class_ok, ok)
        join_at.append(jnp.where(class_ok, t_max, 0))
    for c in range(len(join_at) - 2, -1, -1):
        join_at[c] = jnp.minimum(join_at[c], join_at[c + 1])

    m_ref[...] = jnp.full(m_ref.shape, -jnp.inf, F32)
    acc_ref[...] = jnp.zeros(acc_ref.shape, F32)
    first_final = (qb == 0).astype(jnp.int32)
    active = tuple(hd for hd in range(N_HEADS) if all(hd not in heads for heads, _ in FAR_SKIP_CLASSES))
    pc_0 = pos_tile(0, first_final)
    for hd in active:
        scores(0, hd, 0, pc_0)
    for (heads, _), t_join in zip(FAR_SKIP_CLASSES, join_at):
        pc_join = pos_tile(jnp.minimum(t_join, qb), jnp.where(t_join == 0, first_final, 0))
        for hd in heads:
            scores(0, hd, jnp.minimum(t_join, qb), pc_join)

    def slot_loop(t_from, t_to, heads):
        def trip(i, carry):
            run_slots(t_from + unroll * i, unroll, heads)
            return carry
        lax.fori_loop(0, (t_to - t_from) // unroll, trip, 0)

    t_prev = 0
    for (heads, _), t_join in zip(FAR_SKIP_CLASSES, join_at):
        slot_loop(t_prev, t_join, active)
        active, t_prev = active + heads, t_join
    slot_loop(t_prev, n_slots, active)

    @pl.when((n_slots - t_prev) % unroll != 0)
    def _():
        run_slots(n_slots - 2, 2, active)

    def finish(buf):
        key_i = lax.broadcasted_iota(jnp.int32, (blk, blk), 0)
        qry_i = lax.broadcasted_iota(jnp.int32, (blk, blk), 1)
        vt_own = vt_ref[qb]
        for hd in range(N_HEADS):
            update(hd, jnp.where(key_i <= qry_i, s_ref[buf, hd], MASK_VALUE), vt_own)
        outs = []
        for hd in range(N_HEADS):
            acc = acc_ref[hd]
            outs.append(acc[0:HEAD_DIM, :] / acc[HEAD_DIM:HEAD_DIM + 1, :])
        o_ref[...] = jnp.concatenate(outs, axis=0).T.astype(o_ref.dtype)

    @pl.when(qb % 2 == 1)
    def _():
        phase(0, 1, vt_ref[qb - 1], qb, 1, active)
        finish(1)

    @pl.when(qb % 2 == 0)
    def _():
        finish(0)


def _moba(qt, k, vt, kmean, knorm, pos_const):
    B, S, _ = k.shape
    nb = S // MOBA_BLOCK
    return pl.pallas_call(
        _moba_kernel,
        grid=(B, nb),
        in_specs=[
            pl.BlockSpec((None, D_BRANCH, MOBA_BLOCK), lambda b, i: (b, 0, i)),
            pl.BlockSpec((None, S, D_BRANCH), lambda b, i: (b, 0, 0)),
            pl.BlockSpec((None, nb, D_BRANCH, MOBA_BLOCK), lambda b, i: (b, 0, 0, 0)),
            pl.BlockSpec((S, LANES), lambda b, i: (0, 0)),
            pl.BlockSpec((None, MAX_BLOCKS, D_BRANCH), lambda b, i: (b, 0, 0)),
            pl.BlockSpec((None, MAX_BLOCKS, LANES), lambda b, i: (b, 0, 0)),
        ],
        out_specs=pl.BlockSpec((None, MOBA_BLOCK, D_BRANCH), lambda b, i: (b, i, 0)),
        out_shape=jax.ShapeDtypeStruct((B, S, D_BRANCH), BF16),
        scratch_shapes=[
            pltpu.VMEM((N_HEADS, 2 * LANES, MOBA_BLOCK), BF16),
            pltpu.VMEM((2, N_HEADS, MOBA_BLOCK, MOBA_BLOCK), F32),
            pltpu.VMEM((N_HEADS, 1, MOBA_BLOCK), F32),
            pltpu.VMEM((N_HEADS, HEAD_DIM + _SUM_ROWS, MOBA_BLOCK), F32),
        ],
        compiler_params=pltpu.CompilerParams(
            dimension_semantics=("arbitrary", "arbitrary"), vmem_limit_bytes=VMEM_LIMIT_BYTES),
        name="moba",
    )(qt, k, vt, pos_const, kmean, knorm)


def _merge_out_kernel(x_ref, mixed_ref, gate_c_ref, att_ref, gc_ref, wbc_ref, wout_ref, g_ref, b_ref,
                      o_ref, *, alpha):
    y_c = (att_ref[...].astype(F32) * gc_ref[...].astype(F32)).astype(BF16)
    mixed = mixed_ref[...].astype(F32) + gate_c_ref[...].astype(F32) * _dot(y_c, wbc_ref[...])
    out = _dot(mixed.astype(BF16), wout_ref[...])
    o_ref[...] = _layer_norm(alpha * x_ref[...] + out, g_ref[...], b_ref[...])


def _merge_out(x, mixed_ab, gate_c, att, gc, w_branch, w_out, ln_g, ln_b, *, layer, tm, alpha):
    B, S, D = x.shape
    row_spec = lambda width: pl.BlockSpec((None, tm, width), lambda b, i: (b, i, 0))
    full = lambda shape: pl.BlockSpec(shape, lambda b, i: (0,) * len(shape))
    return pl.pallas_call(
        functools.partial(_merge_out_kernel, alpha=alpha),
        grid=(B, S // tm),
        in_specs=[row_spec(D), row_spec(D_MODEL), row_spec(D_MODEL), row_spec(D_BRANCH), row_spec(D_BRANCH),
                  pl.BlockSpec((None, None, D_BRANCH, D_MODEL), lambda b, i: (layer, 2, 0, 0)),
                  pl.BlockSpec((None, D_MODEL, D_MODEL), lambda b, i: (layer, 0, 0)),
                  full((1, D_MODEL)), full((1, D_MODEL))],
        out_specs=row_spec(D_MODEL),
        out_shape=jax.ShapeDtypeStruct((B, S, D_MODEL), F32),
        compiler_params=pltpu.CompilerParams(
            dimension_semantics=("arbitrary", "arbitrary"), vmem_limit_bytes=VMEM_LIMIT_BYTES),
        name="merge_out",
    )(x, mixed_ab, gate_c, att, gc, w_branch, w_out, ln_g, ln_b)


def _position_constants(S):
    s = jnp.arange(S, dtype=jnp.int32)
    blk, pos = s // MOBA_BLOCK, s % MOBA_BLOCK
    lane = jnp.arange(LANES, dtype=jnp.int32)[None, :]
    pc = (lane == blk[:, None]).astype(F32)
    pc = jnp.where((lane >= _PC_ONE) & (lane < _PC_ONE + N_SPLIT), 1.0, pc)
    pc = jnp.where((lane >= _PC_BLK) & (lane < _PC_BLK + N_SPLIT), (blk * MOBA_BLOCK).astype(F32)[:, None], pc)
    pc = jnp.where((lane >= _PC_POS) & (lane < _PC_POS + N_SPLIT), pos.astype(F32)[:, None], pc)
    return pc.astype(BF16)


def kernel(x, w_in, sg_w, sg_b, v_ln_g, v_ln_b, conv_w, conv_b, cv_ln_g, cv_ln_b, w_branch, w_out, ln_g, ln_b):
    depth = w_in.shape[0]
    B, S, D = x.shape
    nb = S // MOBA_BLOCK
    assert D == D_MODEL and S % MOBA_BLOCK == 0 and nb <= MAX_BLOCKS == _PC_ONE
    alpha = (2 * depth) ** 0.25
    tm = 512
    tm_out = min(1024, S)
    pos_const = _position_constants(S)
    w_in_b = w_in.astype(BF16)
    w_br_b = w_branch.astype(BF16)
    w_out_b = w_out.astype(BF16)
    head_ones = (jnp.arange(D_BRANCH)[:, None] // HEAD_DIM == jnp.arange(LANES)[None, :]).astype(BF16)
    sg_b_bc = jnp.broadcast_to(sg_b[..., None], sg_b.shape + (D_BRANCH // GMLP_GROUPS,))
    row = lambda a, l: a[l][None, :]
    for l in range(depth):
        mixed_ab, gate_c, qt, k, vt, gc, kmean, knorm = _proj_mix(
            x, w_in_b, sg_w[l], sg_b_bc[l], row(v_ln_g, l), row(v_ln_b, l), conv_w[l], row(conv_b, l),
            row(cv_ln_g, l), row(cv_ln_b, l), w_br_b, head_ones, layer=l, tm=tm)
        pad_blocks = lambda a: jnp.pad(a.reshape(B, nb, -1), ((0, 0), (0, MAX_BLOCKS - nb), (0, 0)))
        att = _moba(qt, k, vt, pad_blocks(kmean), pad_blocks(knorm), pos_const)
        x = _merge_out(x, mixed_ab, gate_c, att, gc, w_br_b, w_out_b, row(ln_g, l), row(ln_b, l),
                       layer=l, tm=tm_out, alpha=alpha)
    return x
```

```python
import functools

import numpy as np
import jax
import jax.numpy as jnp
from jax import lax
from jax.experimental import pallas as pl
from jax.experimental.pallas import tpu as pltpu

F32 = jnp.float32
BF16 = jnp.bfloat16

D_MODEL = 1024
D_BRANCH = 512
GMLP_GROUPS = 4
GMLP_CHUNK = 128
CONV_WIDTH = 31
CONV_HALO = 32
CONV_CHUNK = 128
ROW_CHUNK = 64
HEAD_DIM = 64
N_HEADS = 8
MOBA_BLOCK = 256
MOBA_TOPK = 3
N_BRANCH = 3
W_IN_COLS = 10 * D_BRANCH + N_BRANCH * D_MODEL
LN_EPS = 1e-5
MASK_VALUE = -1e30
LANES = 128
SUBLANES = 8
LOG2E = 1.4426950408889634
N_PAIRS = N_HEADS * HEAD_DIM // LANES
MAX_BLOCKS = 32
VMEM_LIMIT_BYTES = 56 * 1024 * 1024
FAR_SKIP_CLASSES = (((4,), 12), ((2, 3), 6), ((0, 1), 2))
FAR_SKIP_LOG2 = -100.0
FAR_SKIP_REL_SLACK = 0.02
FAR_SKIP_ABS_SLACK = 1.0

_C_AU, _C_AV, _C_AG = 0, 512, 1024
_C_BVAL, _C_BGLU, _C_BG = 1536, 2048, 2560
_C_Q, _C_K, _C_V, _C_CG = 3072, 3584, 4096, 4608
_C_MERGE = 5120

N_SPLIT = 3
_PC_ONE, _PC_BLK, _PC_POS = MAX_BLOCKS, MAX_BLOCKS + N_SPLIT, MAX_BLOCKS + 2 * N_SPLIT
_PC_ROWS = 16
_SUM_ROWS = 16


def _sigmoid(x):
    return 0.5 * jnp.tanh(0.5 * x) + 0.5


def _silu(x):
    return x * _sigmoid(x)


def _layer_norm(x, g, b):
    mu = jnp.mean(x, axis=-1, keepdims=True)
    xc = x - mu
    var = jnp.mean(xc * xc, axis=-1, keepdims=True)
    return xc * lax.rsqrt(var + LN_EPS) * g + b


def _dot(a, b):
    return jnp.dot(a, b, preferred_element_type=F32)


def _proj_mix_kernel(x_ref, w_in_ref, sg_w_ref, sg_b_ref, vg_ref, vb_ref, cw_ref, cb_ref,
                     cg_ref, cbeta_ref, wbr_ref, head_ones_ref,
                     mixed_ref, gate_c_ref, qt_ref, k_ref, vt_ref, gc_ref, kmean_ref, knorm_ref,
                     h_ref, xb_ref, conv_buf, shift_buf, vn_ref, y_ref, *, tm):
    i = pl.program_id(1)
    xb_ref[...] = x_ref[...].astype(BF16)
    chunks = [slice(r * ROW_CHUNK, (r + 1) * ROW_CHUNK) for r in range(tm // ROW_CHUNK)]
    cols = lambda c0, width=D_BRANCH: slice(c0, c0 + width)
    c_merge_a, c_merge_b, c_merge_c = _C_MERGE, _C_MERGE + D_MODEL, _C_MERGE + 2 * D_MODEL
    c_z = _C_AV
    c_br_a = _C_AU
    c_conv = _C_BVAL
    c_br_b = _C_BGLU

    def proj(c0, width=D_BRANCH):
        h_ref[:, cols(c0, width)] = _dot(xb_ref[...], w_in_ref[:, cols(c0, width)])

    piece = 2 * LANES
    pieces_of = lambda sections: [c0 + off for c0, width in sections for off in range(0, width, piece)]

    proj(_C_BVAL)
    proj(_C_BGLU)

    @pl.when(i == 0)
    def _():
        conv_buf[0:CONV_HALO, :] = jnp.zeros((CONV_HALO, D_BRANCH), F32)

    for rs in chunks:
        glu = h_ref[rs, cols(_C_BVAL)] * _sigmoid(h_ref[rs, cols(_C_BGLU)])
        conv_buf[CONV_HALO + rs.start:CONV_HALO + rs.stop, :] = glu
    rc = CONV_CHUNK
    first = CONV_HALO - (CONV_WIDTH - 1)

    def conv_chunk(r, lc):
        ls = slice(lc * LANES, (lc + 1) * LANES)
        acc = jnp.zeros((rc, LANES), F32)
        for phase in range(SUBLANES):
            taps = [j for j in range(CONV_WIDTH) if (first + j) % SUBLANES == phase]
            if not taps:
                continue
            offs = [(first + j) // SUBLANES * SUBLANES for j in taps]
            r0 = r * rc + phase
            n = rc + max(offs)
            shift_buf[phase, 0:n, :] = conv_buf[r0:r0 + n, ls]
            window = shift_buf[phase, 0:n, :]
            for j, a in zip(taps, offs):
                acc = acc + window[a:a + rc, :] * cw_ref[j:j + 1, ls]
        h_ref[r * rc:(r + 1) * rc, cols(c_conv + lc * LANES, LANES)] = acc

    fillers = pieces_of(((_C_AV, D_BRANCH), (_C_AU, D_BRANCH), (_C_AG, D_BRANCH), (c_merge_a, D_MODEL)))
    n_regions = tm // rc
    per_region = -(-len(fillers) // n_regions)
    for r in range(n_regions):
        @pl.when(i >= 0)
        def _(r=r):
            for c0 in fillers[r * per_region:(r + 1) * per_region]:
                proj(c0, piece)
            for lc in range(D_BRANCH // LANES):
                conv_chunk(r, lc)
    conv_buf[0:CONV_HALO, :] = conv_buf[tm:tm + CONV_HALO, :]

    late = pieces_of(((_C_BG, D_BRANCH), (c_merge_b, D_MODEL), (_C_CG, D_BRANCH), (c_merge_c, D_MODEL),
                      (_C_Q, D_BRANCH), (_C_K, D_BRANCH), (_C_V, D_BRANCH)))

    def next_piece():
        if late:
            proj(late.pop(0), piece)

    for rs in chunks:
        vn_ref[rs, :] = _layer_norm(h_ref[rs, cols(_C_AV)], vg_ref[...], vb_ref[...]).astype(BF16)
        next_piece()
    row = lax.broadcasted_iota(jnp.int32, (GMLP_CHUNK, GMLP_CHUNK), 0)
    col = lax.broadcasted_iota(jnp.int32, (GMLP_CHUNK, GMLP_CHUNK), 1)
    cgw = D_BRANCH // GMLP_GROUPS
    for g in range(GMLP_GROUPS):
        wm = jnp.where(col <= row, sg_w_ref[g], 0.0).astype(BF16)
        for c in range(tm // GMLP_CHUNK):
            rs = slice(c * GMLP_CHUNK, (c + 1) * GMLP_CHUNK)
            h_ref[rs, cols(c_z + g * cgw, cgw)] = _dot(wm, vn_ref[rs, cols(g * cgw, cgw)]) + sg_b_ref[g]
    for rs in chunks:
        y_a = h_ref[rs, cols(_C_AU)] * h_ref[rs, cols(c_z)] * _silu(h_ref[rs, cols(_C_AG)])
        y_ref[rs, :] = y_a.astype(BF16)
        next_piece()
    h_ref[:, cols(c_br_a, D_MODEL)] = _dot(y_ref[...], wbr_ref[0])

    for rs in chunks:
        cv = _silu(_layer_norm(h_ref[rs, cols(c_conv)] + cb_ref[...], cg_ref[...], cbeta_ref[...]))
        y_ref[rs, :] = (cv * _silu(h_ref[rs, cols(_C_BG)])).astype(BF16)
        gc_ref[rs, :] = _silu(h_ref[rs, cols(_C_CG)]).astype(BF16)
        gate_c_ref[rs, :] = _sigmoid(h_ref[rs, cols(c_merge_c, D_MODEL)]).astype(BF16)
        next_piece()
    while late:
        next_piece()
    h_ref[:, cols(c_br_b, D_MODEL)] = _dot(y_ref[...], wbr_ref[1])
    for rs in chunks:
        h_ref[rs, cols(c_merge_a, D_MODEL)] = (
            _sigmoid(h_ref[rs, cols(c_merge_a, D_MODEL)]) * h_ref[rs, cols(c_br_a, D_MODEL)])

    for rs in chunks:
        k_ref[rs, :] = h_ref[rs, cols(_C_K)].astype(BF16)
    for c in range(tm // MOBA_BLOCK):
        rs = slice(c * MOBA_BLOCK, (c + 1) * MOBA_BLOCK)
        qt_ref[:, rs] = (h_ref[rs, cols(_C_Q)] * (HEAD_DIM ** -0.5 * LOG2E)).T.astype(BF16)
        vt_ref[c] = h_ref[rs, cols(_C_V)].T.astype(BF16)
        kblk = h_ref[rs, cols(_C_K)]
        kmean_ref[c] = jnp.mean(kblk, axis=0, keepdims=True)
        norm2 = _dot((kblk * kblk).astype(BF16), head_ones_ref[...])
        knorm_ref[c] = jnp.max(norm2, axis=0, keepdims=True)
    for rs in chunks:
        gated_b = _sigmoid(h_ref[rs, cols(c_merge_b, D_MODEL)]) * h_ref[rs, cols(c_br_b, D_MODEL)]
        mixed_ref[rs, :] = (h_ref[rs, cols(c_merge_a, D_MODEL)] + gated_b).astype(BF16)


def _proj_mix(x, w_in, sg_w, sg_b_bc, v_ln_g, v_ln_b, conv_w, conv_b, cv_ln_g, cv_ln_b, w_branch, head_ones,
              *, layer, tm):
    B, S, D = x.shape
    nb = S // MOBA_BLOCK
    bpt = tm // MOBA_BLOCK
    grid = (B, S // tm)
    row_spec = lambda width: pl.BlockSpec((None, tm, width), lambda b, i: (b, i, 0))
    full = lambda shape: pl.BlockSpec(shape, lambda b, i: (0,) * len(shape))
    vec = full((1, D_BRANCH))
    return pl.pallas_call(
        functools.partial(_proj_mix_kernel, tm=tm),
        grid=grid,
        in_specs=[
            row_spec(D),
            pl.BlockSpec((None, D, W_IN_COLS), lambda b, i: (layer, 0, 0), pipeline_mode=pl.Buffered(1)),
            full((GMLP_GROUPS, GMLP_CHUNK, GMLP_CHUNK)),
            full((GMLP_GROUPS, GMLP_CHUNK, GMLP_CHUNK)),
            vec, vec,
            full((CONV_WIDTH, D_BRANCH)),
            vec, vec, vec,
            pl.BlockSpec((None, 2, D_BRANCH, D_MODEL), lambda b, i: (layer, 0, 0, 0)),
            full((D_BRANCH, LANES)),
        ],
        out_specs=[row_spec(D_MODEL), row_spec(D_MODEL),
                   pl.BlockSpec((None, D_BRANCH, tm), lambda b, i: (b, 0, i)),
                   row_spec(D_BRANCH),
                   pl.BlockSpec((None, bpt, D_BRANCH, MOBA_BLOCK), lambda b, i: (b, i, 0, 0)),
                   row_spec(D_BRANCH),
                   pl.BlockSpec((None, bpt, 1, D_BRANCH), lambda b, i: (b, i, 0, 0)),
                   pl.BlockSpec((None, bpt, 1, LANES), lambda b, i: (b, i, 0, 0))],
        out_shape=[
            jax.ShapeDtypeStruct((B, S, D_MODEL), BF16),
            jax.ShapeDtypeStruct((B, S, D_MODEL), BF16),
            jax.ShapeDtypeStruct((B, D_BRANCH, S), BF16),
            jax.ShapeDtypeStruct((B, S, D_BRANCH), BF16),
            jax.ShapeDtypeStruct((B, nb, D_BRANCH, MOBA_BLOCK), BF16),
            jax.ShapeDtypeStruct((B, S, D_BRANCH), BF16),
            jax.ShapeDtypeStruct((B, nb, 1, D_BRANCH), F32),
            jax.ShapeDtypeStruct((B, nb, 1, LANES), F32),
        ],
        scratch_shapes=[pltpu.VMEM((tm, W_IN_COLS), F32),
                        pltpu.VMEM((tm, D_MODEL), BF16),
                        pltpu.VMEM((CONV_HALO + tm, D_BRANCH), F32),
                        pltpu.VMEM((SUBLANES, CONV_CHUNK + CONV_HALO, LANES), F32),
                        pltpu.VMEM((tm, D_BRANCH), BF16),
                        pltpu.VMEM((tm, D_BRANCH), BF16)],
        compiler_params=pltpu.CompilerParams(
            dimension_semantics=("arbitrary", "arbitrary"), vmem_limit_bytes=VMEM_LIMIT_BYTES),
        name="proj_mix",
    )(x, w_in, sg_w, sg_b_bc, v_ln_g, v_ln_b, conv_w, conv_b, cv_ln_g, cv_ln_b, w_branch, head_ones)


def _split_bf16(c, parts=N_SPLIT):
    out, rest = [], np.float32(c)
    for _ in range(parts):
        piece = np.float32(rest.astype(BF16))
        out.append(float(piece))
        rest = np.float32(rest - piece)
    return out


def _moba_kernel(qt_ref, k_ref, vt_ref, pc_ref, km_ref, kn_ref, o_ref, qx_ref, s_ref, m_ref, acc_ref):
    qb = pl.program_id(1)
    blk = MOBA_BLOCK
    nbm = MAX_BLOCKS
    rown = lax.broadcasted_iota(jnp.int32, (nbm, blk), 0)
    row_f = lax.broadcasted_iota(jnp.int32, (LANES, blk), 0)
    head_rows = (row_f < HEAD_DIM, row_f >= HEAD_DIM)
    qbv = jnp.full((nbm, blk), qb, jnp.int32)
    rowp = lax.broadcasted_iota(jnp.int32, (_PC_ROWS, blk), 0) + nbm
    t_pos = (qbv[0:_PC_ROWS, :] * blk + lax.broadcasted_iota(jnp.int32, (_PC_ROWS, blk), 1)).astype(F32)
    neg_inf = jnp.float32(-jnp.inf)
    zeros_tail = jnp.zeros((LANES - nbm - _PC_ROWS, blk), F32)
    ones_rows = jnp.ones((_SUM_ROWS, blk), BF16)

    for p in range(N_PAIRS):
        qtp = qt_ref[p * LANES:(p + 1) * LANES, :]
        km = km_ref[:, p * LANES:(p + 1) * LANES]
        km_hi = km.astype(BF16)
        km_lo = (km - km_hi.astype(F32)).astype(BF16)
        for h in range(2):
            c_parts = _split_bf16(2.0 ** (-(2 * p + h + 1) * 8.0 / N_HEADS) * LOG2E)
            qh = jnp.where(head_rows[h], qtp, jnp.zeros_like(qtp))
            gate = _dot(km_hi, qh) + _dot(km_lo, qh)
            g = jnp.where(rown < qbv, gate, neg_inf)
            selected = rown < 0
            for r in range(MOBA_TOPK):
                mx = jnp.max(g, axis=0, keepdims=True)
                idx = jnp.min(jnp.where(g == mx, rown, nbm), axis=0, keepdims=True)
                pick = rown == idx
                selected = jnp.logical_or(selected, jnp.logical_and(pick, qbv > r))
                g = jnp.where(pick, neg_inf, g)
            mask_rows = jnp.where(selected, 0.0, MASK_VALUE)
            bias_t = -sum(c_parts) * t_pos
            bias_t1 = bias_t.astype(BF16).astype(F32)
            bias_t2 = (bias_t - bias_t1).astype(BF16).astype(F32)
            pos_rows = jnp.zeros((_PC_ROWS, blk), F32)
            for i, piece in enumerate((bias_t1, bias_t2, bias_t - bias_t1 - bias_t2)):
                pos_rows = jnp.where(rowp == _PC_ONE + i, piece, pos_rows)
            for i, c_i in enumerate(c_parts):
                pos_rows = jnp.where(jnp.logical_or(rowp == _PC_BLK + i, rowp == _PC_POS + i), c_i, pos_rows)
            extra = jnp.concatenate([mask_rows, pos_rows, zeros_tail], axis=0)
            qx_ref[2 * p + h] = jnp.concatenate([qh, extra.astype(BF16)], axis=0)

    def values(vt_blk, hd):
        return jnp.concatenate([vt_blk[hd * HEAD_DIM:(hd + 1) * HEAD_DIM, :], ones_rows], axis=0)

    lane_pc = lax.broadcasted_iota(jnp.int32, (blk, LANES), 1)
    n_slots = qb // 2 * 2

    def block_rows(j):
        return pl.ds(j * blk if isinstance(j, int) else pl.multiple_of(j * blk, blk), blk)

    def pos_tile(j, final):
        pc_j = pc_ref[block_rows(j), :]
        drop = jnp.logical_and(jnp.full((blk, LANES), final, jnp.int32) != 0, lane_pc < nbm)
        return jnp.where(drop, jnp.zeros_like(pc_j), pc_j)

    def update(hd, s, vt_blk):
        m_old = m_ref[hd]
        m_new = jnp.maximum(m_old, jnp.max(s, axis=0, keepdims=True))
        alpha = jnp.exp2(m_old - m_new)
        pexp = jnp.exp2(s - m_new)
        acc_ref[hd] = alpha * acc_ref[hd] + _dot(values(vt_blk, hd), pexp.astype(BF16))
        m_ref[hd] = m_new

    def scores(dst, hd, j, pc_j):
        kk = jnp.concatenate([k_ref[block_rows(j), (hd // 2) * LANES:(hd // 2 + 1) * LANES], pc_j], axis=1)
        s_ref[dst, hd] = _dot(kk, qx_ref[hd])

    def phase(src, dst, vt_blk, j_next, final_next, heads):
        lead = 2
        pc_n = pos_tile(j_next, final_next)
        for step in range(len(heads) + lead):
            if step < len(heads):
                scores(dst, heads[step], j_next, pc_n)
            if step >= lead:
                update(heads[step - lead], s_ref[src, heads[step - lead]], vt_blk)

    def run_slots(t0, count, heads):
        for u in range(count):
            t_next = t0 + u + 1
            final_next = (t_next == qb).astype(jnp.int32)
            phase(u % 2, 1 - u % 2, vt_ref[t0 + u], t_next, final_next, heads)

    unroll = 4
    key_dist = ((qbv - rown) * blk + lax.broadcasted_iota(jnp.int32, (nbm, blk), 1) - (blk - 1)).astype(F32)
    kt_own = {}
    join_at = []
    for heads, window in FAR_SKIP_CLASSES:
        t_max = jnp.maximum(qb - window, 0) // unroll * unroll
        class_ok = None
        for hd in heads:
            p, h = divmod(hd, 2)
            if p not in kt_own:
                kt_own[p] = k_ref[block_rows(qb), p * LANES:(p + 1) * LANES].astype(F32).T
            c_hd = 2.0 ** (-(hd + 1) * 8.0 / N_HEADS) * LOG2E
            qf = qt_ref[hd * HEAD_DIM:(hd + 1) * HEAD_DIM, :].astype(F32)
            q_norm = jnp.sqrt(jnp.sum(qf * qf, axis=0, keepdims=True))
            diag = jnp.sum(qf * kt_own[p][h * HEAD_DIM:(h + 1) * HEAD_DIM, :], axis=0, keepdims=True)
            k_norm = jnp.sqrt(kn_ref[:, hd:hd + 1])
            slack = FAR_SKIP_REL_SLACK * q_norm * jnp.max(k_norm, axis=0, keepdims=True) + FAR_SKIP_ABS_SLACK
            bound = (1.0 + FAR_SKIP_REL_SLACK) * k_norm * q_norm - c_hd * key_dist - diag + slack
            masked = jnp.where(rown < t_max, bound, neg_inf)
            worst = jnp.max(jnp.max(masked, axis=0, keepdims=True), axis=1, keepdims=True)[0, 0]
            ok = worst < FAR_SKIP_LOG2
            class_ok = ok if class_ok is None else jnp.logical_and(class_ok, ok)
        join_at.append(jnp.where(class_ok, t_max, 0))
    for c in range(len(join_at) - 2, -1, -1):
        join_at[c] = jnp.minimum(join_at[c], join_at[c + 1])

    m_ref[...] = jnp.full(m_ref.shape, -jnp.inf, F32)
    acc_ref[...] = jnp.zeros(acc_ref.shape, F32)
    first_final = (qb == 0).astype(jnp.int32)
    active = tuple(hd for hd in range(N_HEADS) if all(hd not in heads for heads, _ in FAR_SKIP_CLASSES))
    pc_0 = pos_tile(0, first_final)
    for hd in active:
        scores(0, hd, 0, pc_0)
    for (heads, _), t_join in zip(FAR_SKIP_CLASSES, join_at):
        pc_join = pos_tile(jnp.minimum(t_join, qb), jnp.where(t_join == 0, first_final, 0))
        for hd in heads:
            scores(0, hd, jnp.minimum(t_join, qb), pc_join)

    def slot_loop(t_from, t_to, heads):
        def trip(i, carry):
            run_slots(t_from + unroll * i, unroll, heads)
            return carry
        lax.fori_loop(0, (t_to - t_from) // unroll, trip, 0)

    t_prev = 0
    for (heads, _), t_join in zip(FAR_SKIP_CLASSES, join_at):
        slot_loop(t_prev, t_join, active)
        active, t_prev = active + heads, t_join
    slot_loop(t_prev, n_slots, active)

    @pl.when((n_slots - t_prev) % unroll != 0)
    def _():
        run_slots(n_slots - 2, 2, active)

    def finish(buf):
        key_i = lax.broadcasted_iota(jnp.int32, (blk, blk), 0)
        qry_i = lax.broadcasted_iota(jnp.int32, (blk, blk), 1)
        vt_own = vt_ref[qb]
        for hd in range(N_HEADS):
            update(hd, jnp.where(key_i <= qry_i, s_ref[buf, hd], MASK_VALUE), vt_own)
        outs = []
        for hd in range(N_HEADS):
            acc = acc_ref[hd]
            outs.append(acc[0:HEAD_DIM, :] / acc[HEAD_DIM:HEAD_DIM + 1, :])
        o_ref[...] = jnp.concatenate(outs, axis=0).T.astype(o_ref.dtype)

    @pl.when(qb % 2 == 1)
    def _():
        phase(0, 1, vt_ref[qb - 1], qb, 1, active)
        finish(1)

    @pl.when(qb % 2 == 0)
    def _():
        finish(0)


def _moba(qt, k, vt, kmean, knorm, pos_const):
    B, S, _ = k.shape
    nb = S // MOBA_BLOCK
    return pl.pallas_call(
        _moba_kernel,
        grid=(B, nb),
        in_specs=[
            pl.BlockSpec((None, D_BRANCH, MOBA_BLOCK), lambda b, i: (b, 0, i)),
            pl.BlockSpec((None, S, D_BRANCH), lambda b, i: (b, 0, 0)),
            pl.BlockSpec((None, nb, D_BRANCH, MOBA_BLOCK), lambda b, i: (b, 0, 0, 0)),
            pl.BlockSpec((S, LANES), lambda b, i: (0, 0)),
            pl.BlockSpec((None, MAX_BLOCKS, D_BRANCH), lambda b, i: (b, 0, 0)),
            pl.BlockSpec((None, MAX_BLOCKS, LANES), lambda b, i: (b, 0, 0)),
        ],
        out_specs=pl.BlockSpec((None, MOBA_BLOCK, D_BRANCH), lambda b, i: (b, i, 0)),
        out_shape=jax.ShapeDtypeStruct((B, S, D_BRANCH), BF16),
        scratch_shapes=[
            pltpu.VMEM((N_HEADS, 2 * LANES, MOBA_BLOCK), BF16),
            pltpu.VMEM((2, N_HEADS, MOBA_BLOCK, MOBA_BLOCK), F32),
            pltpu.VMEM((N_HEADS, 1, MOBA_BLOCK), F32),
            pltpu.VMEM((N_HEADS, HEAD_DIM + _SUM_ROWS, MOBA_BLOCK), F32),
        ],
        compiler_params=pltpu.CompilerParams(
            dimension_semantics=("arbitrary", "arbitrary"), vmem_limit_bytes=VMEM_LIMIT_BYTES),
        name="moba",
    )(qt, k, vt, pos_const, kmean, knorm)


def _merge_out_kernel(x_ref, mixed_ref, gate_c_ref, att_ref, gc_ref, wbc_ref, wout_ref, g_ref, b_ref,
                      o_ref, *, alpha):
    y_c = (att_ref[...].astype(F32) * gc_ref[...].astype(F32)).astype(BF16)
    mixed = mixed_ref[...].astype(F32) + gate_c_ref[...].astype(F32) * _dot(y_c, wbc_ref[...])
    out = _dot(mixed.astype(BF16), wout_ref[...])
    o_ref[...] = _layer_norm(alpha * x_ref[...] + out, g_ref[...], b_ref[...])


def _merge_out(x, mixed_ab, gate_c, att, gc, w_branch, w_out, ln_g, ln_b, *, layer, tm, alpha):
    B, S, D = x.shape
    row_spec = lambda width: pl.BlockSpec((None, tm, width), lambda b, i: (b, i, 0))
    full = lambda shape: pl.BlockSpec(shape, lambda b, i: (0,) * len(shape))
    return pl.pallas_call(
        functools.partial(_merge_out_kernel, alpha=alpha),
        grid=(B, S // tm),
        in_specs=[row_spec(D), row_spec(D_MODEL), row_spec(D_MODEL), row_spec(D_BRANCH), row_spec(D_BRANCH),
                  pl.BlockSpec((None, None, D_BRANCH, D_MODEL), lambda b, i: (layer, 2, 0, 0)),
                  pl.BlockSpec((None, D_MODEL, D_MODEL), lambda b, i: (layer, 0, 0)),
                  full((1, D_MODEL)), full((1, D_MODEL))],
        out_specs=row_spec(D_MODEL),
        out_shape=jax.ShapeDtypeStruct((B, S, D_MODEL), F32),
        compiler_params=pltpu.CompilerParams(
            dimension_semantics=("arbitrary", "arbitrary"), vmem_limit_bytes=VMEM_LIMIT_BYTES),
        name="merge_out",
    )(x, mixed_ab, gate_c, att, gc, w_branch, w_out, ln_g, ln_b)


def _position_constants(S):
    s = jnp.arange(S, dtype=jnp.int32)
    blk, pos = s // MOBA_BLOCK, s % MOBA_BLOCK
    lane = jnp.arange(LANES, dtype=jnp.int32)[None, :]
    pc = (lane == blk[:, None]).astype(F32)
    pc = jnp.where((lane >= _PC_ONE) & (lane < _PC_ONE + N_SPLIT), 1.0, pc)
    pc = jnp.where((lane >= _PC_BLK) & (lane < _PC_BLK + N_SPLIT), (blk * MOBA_BLOCK).astype(F32)[:, None], pc)
    pc = jnp.where((lane >= _PC_POS) & (lane < _PC_POS + N_SPLIT), pos.astype(F32)[:, None], pc)
    return pc.astype(BF16)


def kernel(x, w_in, sg_w, sg_b, v_ln_g, v_ln_b, conv_w, conv_b, cv_ln_g, cv_ln_b, w_branch, w_out, ln_g, ln_b):
    depth = w_in.shape[0]
    B, S, D = x.shape
    nb = S // MOBA_BLOCK
    assert D == D_MODEL and S % MOBA_BLOCK == 0 and nb <= MAX_BLOCKS == _PC_ONE
    alpha = (2 * depth) ** 0.25
    tm = 512
    tm_out = min(1024, S)
    pos_const = _position_constants(S)
    w_in_b = w_in.astype(BF16)
    w_br_b = w_branch.astype(BF16)
    w_out_b = w_out.astype(BF16)
    head_ones = (jnp.arange(D_BRANCH)[:, None] // HEAD_DIM == jnp.arange(LANES)[None, :]).astype(BF16)
    sg_b_bc = jnp.broadcast_to(sg_b[..., None], sg_b.shape + (D_BRANCH // GMLP_GROUPS,))
    row = lambda a, l: a[l][None, :]
    for l in range(depth):
        mixed_ab, gate_c, qt, k, vt, gc, kmean, knorm = _proj_mix(
            x, w_in_b, sg_w[l], sg_b_bc[l], row(v_ln_g, l), row(v_ln_b, l), conv_w[l], row(conv_b, l),
            row(cv_ln_g, l), row(cv_ln_b, l), w_br_b, head_ones, layer=l, tm=tm)
        pad_blocks = lambda a: jnp.pad(a.reshape(B, nb, -1), ((0, 0), (0, MAX_BLOCKS - nb), (0, 0)))
        att = _moba(qt, k, vt, pad_blocks(kmean), pad_blocks(knorm), pos_const)
        x = _merge_out(x, mixed_ab, gate_c, att, gc, w_br_b, w_out_b, row(ln_g, l), row(ln_b, l),
                       layer=l, tm=tm_out, alpha=alpha)
    return x
```

```python
import functools

import numpy as np
import jax
import jax.numpy as jnp
from jax import lax
from jax.experimental import pallas as pl
from jax.experimental.pallas import tpu as pltpu

F32 = jnp.float32
BF16 = jnp.bfloat16

D_MODEL = 1024
D_BRANCH = 512
GMLP_GROUPS = 4
GMLP_CHUNK = 128
CONV_WIDTH = 31
CONV_HALO = 32
CONV_CHUNK = 128
ROW_CHUNK = 64
HEAD_DIM = 64
N_HEADS = 8
MOBA_BLOCK = 256
MOBA_TOPK = 3
N_BRANCH = 3
W_IN_COLS = 10 * D_BRANCH + N_BRANCH * D_MODEL
LN_EPS = 1e-5
MASK_VALUE = -1e30
LANES = 128
SUBLANES = 8
LOG2E = 1.4426950408889634
N_PAIRS = N_HEADS * HEAD_DIM // LANES
MAX_BLOCKS = 32
VMEM_LIMIT_BYTES = 56 * 1024 * 1024
FAR_SKIP_CLASSES = (((4,), 12), ((2, 3), 6), ((0, 1), 2))
FAR_SKIP_LOG2 = -100.0
FAR_SKIP_REL_SLACK = 0.02
FAR_SKIP_ABS_SLACK = 1.0

_C_AU, _C_AV, _C_AG = 0, 512, 1024
_C_BVAL, _C_BGLU, _C_BG = 1536, 2048, 2560
_C_Q, _C_K, _C_V, _C_CG = 3072, 3584, 4096, 4608
_C_MERGE = 5120

N_SPLIT = 3
_PC_ONE, _PC_BLK, _PC_POS = MAX_BLOCKS, MAX_BLOCKS + N_SPLIT, MAX_BLOCKS + 2 * N_SPLIT
_PC_ROWS = 16
_SUM_ROWS = 16


def _sigmoid(x):
    return 0.5 * jnp.tanh(0.5 * x) + 0.5


def _silu(x):
    return x * _sigmoid(x)


def _layer_norm(x, g, b):
    mu = jnp.mean(x, axis=-1, keepdims=True)
    xc = x - mu
    var = jnp.mean(xc * xc, axis=-1, keepdims=True)
    return xc * lax.rsqrt(var + LN_EPS) * g + b


def _dot(a, b):
    return jnp.dot(a, b, preferred_element_type=F32)


def _proj_mix_kernel(x_ref, w_in_ref, sg_w_ref, sg_b_ref, vg_ref, vb_ref, cw_ref, cb_ref,
                     cg_ref, cbeta_ref, wbr_ref, head_ones_ref,
                     mixed_ref, gate_c_ref, qt_ref, k_ref, vt_ref, gc_ref, kmean_ref, knorm_ref,
                     h_ref, xb_ref, conv_buf, shift_buf, vn_ref, y_ref, *, tm):
    i = pl.program_id(1)
    xb_ref[...] = x_ref[...].astype(BF16)
    chunks = [slice(r * ROW_CHUNK, (r + 1) * ROW_CHUNK) for r in range(tm // ROW_CHUNK)]
    cols = lambda c0, width=D_BRANCH: slice(c0, c0 + width)
    c_merge_a, c_merge_b, c_merge_c = _C_MERGE, _C_MERGE + D_MODEL, _C_MERGE + 2 * D_MODEL
    c_z = _C_AV
    c_br_a = _C_AU
    c_conv = _C_BVAL
    c_br_b = _C_BGLU

    def proj(c0, width=D_BRANCH):
        h_ref[:, cols(c0, width)] = _dot(xb_ref[...], w_in_ref[:, cols(c0, width)])

    piece = 2 * LANES
    pieces_of = lambda sections: [c0 + off for c0, width in sections for off in range(0, width, piece)]

    proj(_C_BVAL)
    proj(_C_BGLU)

    @pl.when(i == 0)
    def _():
        conv_buf[0:CONV_HALO, :] = jnp.zeros((CONV_HALO, D_BRANCH), F32)

    for rs in chunks:
        glu = h_ref[rs, cols(_C_BVAL)] * _sigmoid(h_ref[rs, cols(_C_BGLU)])
        conv_buf[CONV_HALO + rs.start:CONV_HALO + rs.stop, :] = glu
    rc = CONV_CHUNK
    first = CONV_HALO - (CONV_WIDTH - 1)

    def conv_chunk(r, lc):
        ls = slice(lc * LANES, (lc + 1) * LANES)
        acc = jnp.zeros((rc, LANES), F32)
        for phase in range(SUBLANES):
            taps = [j for j in range(CONV_WIDTH) if (first + j) % SUBLANES == phase]
            if not taps:
                continue
            offs = [(first + j) // SUBLANES * SUBLANES for j in taps]
            r0 = r * rc + phase
            n = rc + max(offs)
            shift_buf[phase, 0:n, :] = conv_buf[r0:r0 + n, ls]
            window = shift_buf[phase, 0:n, :]
            for j, a in zip(taps, offs):
                acc = acc + window[a:a + rc, :] * cw_ref[j:j + 1, ls]
        h_ref[r * rc:(r + 1) * rc, cols(c_conv + lc * LANES, LANES)] = acc

    fillers = pieces_of(((_C_AV, D_BRANCH), (_C_AU, D_BRANCH), (_C_AG, D_BRANCH), (c_merge_a, D_MODEL)))
    n_regions = tm // rc
    per_region = -(-len(fillers) // n_regions)
    for r in range(n_regions):
        @pl.when(i >= 0)
        def _(r=r):
            for c0 in fillers[r * per_region:(r + 1) * per_region]:
                proj(c0, piece)
            for lc in range(D_BRANCH // LANES):
                conv_chunk(r, lc)
    conv_buf[0:CONV_HALO, :] = conv_buf[tm:tm + CONV_HALO, :]

    late = pieces_of(((_C_BG, D_BRANCH), (c_merge_b, D_MODEL), (_C_CG, D_BRANCH), (c_merge_c, D_MODEL),
                      (_C_Q, D_BRANCH), (_C_K, D_BRANCH), (_C_V, D_BRANCH)))

    def next_piece():
        if late:
            proj(late.pop(0), piece)

    for rs in chunks:
        vn_ref[rs, :] = _layer_norm(h_ref[rs, cols(_C_AV)], vg_ref[...], vb_ref[...]).astype(BF16)
        next_piece()
    row = lax.broadcasted_iota(jnp.int32, (GMLP_CHUNK, GMLP_CHUNK), 0)
    col = lax.broadcasted_iota(jnp.int32, (GMLP_CHUNK, GMLP_CHUNK), 1)
    cgw = D_BRANCH // GMLP_GROUPS
    for g in range(GMLP_GROUPS):
        wm = jnp.where(col <= row, sg_w_ref[g], 0.0).astype(BF16)
        for c in range(tm // GMLP_CHUNK):
            rs = slice(c * GMLP_CHUNK, (c + 1) * GMLP_CHUNK)
            h_ref[rs, cols(c_z + g * cgw, cgw)] = _dot(wm, vn_ref[rs, cols(g * cgw, cgw)]) + sg_b_ref[g]
    for rs in chunks:
        y_a = h_ref[rs, cols(_C_AU)] * h_ref[rs, cols(c_z)] * _silu(h_ref[rs, cols(_C_AG)])
        y_ref[rs, :] = y_a.astype(BF16)
        next_piece()
    h_ref[:, cols(c_br_a, D_MODEL)] = _dot(y_ref[...], wbr_ref[0])

    for rs in chunks:
        cv = _silu(_layer_norm(h_ref[rs, cols(c_conv)] + cb_ref[...], cg_ref[...], cbeta_ref[...]))
        y_ref[rs, :] = (cv * _silu(h_ref[rs, cols(_C_BG)])).astype(BF16)
        gc_ref[rs, :] = _silu(h_ref[rs, cols(_C_CG)]).astype(BF16)
        gate_c_ref[rs, :] = _sigmoid(h_ref[rs, cols(c_merge_c, D_MODEL)]).astype(BF16)
        next_piece()
    while late:
        next_piece()
    h_ref[:, cols(c_br_b, D_MODEL)] = _dot(y_ref[...], wbr_ref[1])
    for rs in chunks:
        h_ref[rs, cols(c_merge_a, D_MODEL)] = (
            _sigmoid(h_ref[rs, cols(c_merge_a, D_MODEL)]) * h_ref[rs, cols(c_br_a, D_MODEL)])

    for rs in chunks:
        k_ref[rs, :] = h_ref[rs, cols(_C_K)].astype(BF16)
    for c in range(tm // MOBA_BLOCK):
        rs = slice(c * MOBA_BLOCK, (c + 1) * MOBA_BLOCK)
        qt_ref[:, rs] = (h_ref[rs, cols(_C_Q)] * (HEAD_DIM ** -0.5 * LOG2E)).T.astype(BF16)
        vt_ref[c] = h_ref[rs, cols(_C_V)].T.astype(BF16)
        kblk = h_ref[rs, cols(_C_K)]
        kmean_ref[c] = jnp.mean(kblk, axis=0, keepdims=True)
        norm2 = _dot((kblk * kblk).astype(BF16), head_ones_ref[...])
        knorm_ref[c] = jnp.max(norm2, axis=0, keepdims=True)
    for rs in chunks:
        gated_b = _sigmoid(h_ref[rs, cols(c_merge_b, D_MODEL)]) * h_ref[rs, cols(c_br_b, D_MODEL)]
        mixed_ref[rs, :] = (h_ref[rs, cols(c_merge_a, D_MODEL)] + gated_b).astype(BF16)


def _proj_mix(x, w_in, sg_w, sg_b_bc, v_ln_g, v_ln_b, conv_w, conv_b, cv_ln_g, cv_ln_b, w_branch, head_ones,
              *, layer, tm):
    B, S, D = x.shape
    nb = S // MOBA_BLOCK
    bpt = tm // MOBA_BLOCK
    grid = (B, S // tm)
    row_spec = lambda width: pl.BlockSpec((None, tm, width), lambda b, i: (b, i, 0))
    full = lambda shape: pl.BlockSpec(shape, lambda b, i: (0,) * len(shape))
    vec = full((1, D_BRANCH))
    return pl.pallas_call(
        functools.partial(_proj_mix_kernel, tm=tm),
        grid=grid,
        in_specs=[
            row_spec(D),
            pl.BlockSpec((None, D, W_IN_COLS), lambda b, i: (layer, 0, 0), pipeline_mode=pl.Buffered(1)),
            full((GMLP_GROUPS, GMLP_CHUNK, GMLP_CHUNK)),
            full((GMLP_GROUPS, GMLP_CHUNK, GMLP_CHUNK)),
            vec, vec,
            full((CONV_WIDTH, D_BRANCH)),
            vec, vec, vec,
            pl.BlockSpec((None, 2, D_BRANCH, D_MODEL), lambda b, i: (layer, 0, 0, 0)),
            full((D_BRANCH, LANES)),
        ],
        out_specs=[row_spec(D_MODEL), row_spec(D_MODEL),
                   pl.BlockSpec((None, D_BRANCH, tm), lambda b, i: (b, 0, i)),
                   row_spec(D_BRANCH),
                   pl.BlockSpec((None, bpt, D_BRANCH, MOBA_BLOCK), lambda b, i: (b, i, 0, 0)),
                   row_spec(D_BRANCH),
                   pl.BlockSpec((None, bpt, 1, D_BRANCH), lambda b, i: (b, i, 0, 0)),
                   pl.BlockSpec((None, bpt, 1, LANES), lambda b, i: (b, i, 0, 0))],
        out_shape=[
            jax.ShapeDtypeStruct((B, S, D_MODEL), BF16),
            jax.ShapeDtypeStruct((B, S, D_MODEL), BF16),
            jax.ShapeDtypeStruct((B, D_BRANCH, S), BF16),
            jax.ShapeDtypeStruct((B, S, D_BRANCH), BF16),
            jax.ShapeDtypeStruct((B, nb, D_BRANCH, MOBA_BLOCK), BF16),
            jax.ShapeDtypeStruct((B, S, D_BRANCH), BF16),
            jax.ShapeDtypeStruct((B, nb, 1, D_BRANCH), F32),
            jax.ShapeDtypeStruct((B, nb, 1, LANES), F32),
        ],
        scratch_shapes=[pltpu.VMEM((tm, W_IN_COLS), F32),
                        pltpu.VMEM((tm, D_MODEL), BF16),
                        pltpu.VMEM((CONV_HALO + tm, D_BRANCH), F32),
                        pltpu.VMEM((SUBLANES, CONV_CHUNK + CONV_HALO, LANES), F32),
                        pltpu.VMEM((tm, D_BRANCH), BF16),
                        pltpu.VMEM((tm, D_BRANCH), BF16)],
        compiler_params=pltpu.CompilerParams(
            dimension_semantics=("arbitrary", "arbitrary"), vmem_limit_bytes=VMEM_LIMIT_BYTES),
        name="proj_mix",
    )(x, w_in, sg_w, sg_b_bc, v_ln_g, v_ln_b, conv_w, conv_b, cv_ln_g, cv_ln_b, w_branch, head_ones)


def _split_bf16(c, parts=N_SPLIT):
    out, rest = [], np.float32(c)
    for _ in range(parts):
        piece = np.float32(rest.astype(BF16))
        out.append(float(piece))
        rest = np.float32(rest - piece)
    return out


def _moba_kernel(qt_ref, k_ref, vt_ref, pc_ref, km_ref, kn_ref, o_ref, qx_ref, s_ref, m_ref, acc_ref):
    qb = pl.program_id(1)
    blk = MOBA_BLOCK
    nbm = MAX_BLOCKS
    rown = lax.broadcasted_iota(jnp.int32, (nbm, blk), 0)
    row_f = lax.broadcasted_iota(jnp.int32, (LANES, blk), 0)
    head_rows = (row_f < HEAD_DIM, row_f >= HEAD_DIM)
    qbv = jnp.full((nbm, blk), qb, jnp.int32)
    rowp = lax.broadcasted_iota(jnp.int32, (_PC_ROWS, blk), 0) + nbm
    t_pos = (qbv[0:_PC_ROWS, :] * blk + lax.broadcasted_iota(jnp.int32, (_PC_ROWS, blk), 1)).astype(F32)
    neg_inf = jnp.float32(-jnp.inf)
    zeros_tail = jnp.zeros((LANES - nbm - _PC_ROWS, blk), F32)
    ones_rows = jnp.ones((_SUM_ROWS, blk), BF16)

    for p in range(N_PAIRS):
        qtp = qt_ref[p * LANES:(p + 1) * LANES, :]
        km = km_ref[:, p * LANES:(p + 1) * LANES]
        km_hi = km.astype(BF16)
        km_lo = (km - km_hi.astype(F32)).astype(BF16)
        for h in range(2):
            c_parts = _split_bf16(2.0 ** (-(2 * p + h + 1) * 8.0 / N_HEADS) * LOG2E)
            qh = jnp.where(head_rows[h], qtp, jnp.zeros_like(qtp))
            gate = _dot(km_hi, qh) + _dot(km_lo, qh)
            g = jnp.where(rown < qbv, gate, neg_inf)
            selected = rown < 0
            for r in range(MOBA_TOPK):
                mx = jnp.max(g, axis=0, keepdims=True)
                idx = jnp.min(jnp.where(g == mx, rown, nbm), axis=0, keepdims=True)
                pick = rown == idx
                selected = jnp.logical_or(selected, jnp.logical_and(pick, qbv > r))
                g = jnp.where(pick, neg_inf, g)
            mask_rows = jnp.where(selected, 0.0, MASK_VALUE)
            bias_t = -sum(c_parts) * t_pos
            bias_t1 = bias_t.astype(BF16).astype(F32)
            bias_t2 = (bias_t - bias_t1).astype(BF16).astype(F32)
            pos_rows = jnp.zeros((_PC_ROWS, blk), F32)
            for i, piece in enumerate((bias_t1, bias_t2, bias_t - bias_t1 - bias_t2)):
                pos_rows = jnp.where(rowp == _PC_ONE + i, piece, pos_rows)
            for i, c_i in enumerate(c_parts):
                pos_rows = jnp.where(jnp.logical_or(rowp == _PC_BLK + i, rowp == _PC_POS + i), c_i, pos_rows)
            extra = jnp.concatenate([mask_rows, pos_rows, zeros_tail], axis=0)
            qx_ref[2 * p + h] = jnp.concatenate([qh, extra.astype(BF16)], axis=0)

    def values(vt_blk, hd):
        return jnp.concatenate([vt_blk[hd * HEAD_DIM:(hd + 1) * HEAD_DIM, :], ones_rows], axis=0)

    lane_pc = lax.broadcasted_iota(jnp.int32, (blk, LANES), 1)
    n_slots = qb // 2 * 2

    def block_rows(j):
        return pl.ds(j * blk if isinstance(j, int) else pl.multiple_of(j * blk, blk), blk)

    def pos_tile(j, final):
        pc_j = pc_ref[block_rows(j), :]
        drop = jnp.logical_and(jnp.full((blk, LANES), final, jnp.int32) != 0, lane_pc < nbm)
        return jnp.where(drop, jnp.zeros_like(pc_j), pc_j)

    def update(hd, s, vt_blk):
        m_old = m_ref[hd]
        m_new = jnp.maximum(m_old, jnp.max(s, axis=0, keepdims=True))
        alpha = jnp.exp2(m_old - m_new)
        pexp = jnp.exp2(s - m_new)
        acc_ref[hd] = alpha * acc_ref[hd] + _dot(values(vt_blk, hd), pexp.astype(BF16))
        m_ref[hd] = m_new

    def scores(dst, hd, j, pc_j):
        kk = jnp.concatenate([k_ref[block_rows(j), (hd // 2) * LANES:(hd // 2 + 1) * LANES], pc_j], axis=1)
        s_ref[dst, hd] = _dot(kk, qx_ref[hd])

    def phase(src, dst, vt_blk, j_next, final_next, heads):
        lead = 2
        pc_n = pos_tile(j_next, final_next)
        for step in range(len(heads) + lead):
            if step < len(heads):
                scores(dst, heads[step], j_next, pc_n)
            if step >= lead:
                update(heads[step - lead], s_ref[src, heads[step - lead]], vt_blk)

    def run_slots(t0, count, heads):
        for u in range(count):
            t_next = t0 + u + 1
            final_next = (t_next == qb).astype(jnp.int32)
            phase(u % 2, 1 - u % 2, vt_ref[t0 + u], t_next, final_next, heads)

    unroll = 4
    key_dist = ((qbv - rown) * blk + lax.broadcasted_iota(jnp.int32, (nbm, blk), 1) - (blk - 1)).astype(F32)
    kt_own = {}
    join_at = []
    for heads, window in FAR_SKIP_CLASSES:
        t_max = jnp.maximum(qb - window, 0) // unroll * unroll
        class_ok = None
        for hd in heads:
            p, h = divmod(hd, 2)
            if p not in kt_own:
                kt_own[p] = k_ref[block_rows(qb), p * LANES:(p + 1) * LANES].astype(F32).T
            c_hd = 2.0 ** (-(hd + 1) * 8.0 / N_HEADS) * LOG2E
            qf = qt_ref[hd * HEAD_DIM:(hd + 1) * HEAD_DIM, :].astype(F32)
            q_norm = jnp.sqrt(jnp.sum(qf * qf, axis=0, keepdims=True))
            diag = jnp.sum(qf * kt_own[p][h * HEAD_DIM:(h + 1) * HEAD_DIM, :], axis=0, keepdims=True)
            k_norm = jnp.sqrt(kn_ref[:, hd:hd + 1])
            slack = FAR_SKIP_REL_SLACK * q_norm * jnp.max(k_norm, axis=0, keepdims=True) + FAR_SKIP_ABS_SLACK
            bound = (1.0 + FAR_SKIP_REL_SLACK) * k_norm * q_norm - c_hd * key_dist - diag + slack
            masked = jnp.where(rown < t_max, bound, neg_inf)
            worst = jnp.max(jnp.max(masked, axis=0, keepdims=True), axis=1, keepdims=True)[0, 0]
            ok = worst < FAR_SKIP_LOG2
            class_ok = ok if class_ok is None else jnp.logical_and(class_ok, ok)
        join_at.append(jnp.where(class_ok, t_max, 0))
    for c in range(len(join_at) - 2, -1, -1):
        join_at[c] = jnp.minimum(join_at[c], join_at[c + 1])

    m_ref[...] = jnp.full(m_ref.shape, -jnp.inf, F32)
    acc_ref[...] = jnp.zeros(acc_ref.shape, F32)
    first_final = (qb == 0).astype(jnp.int32)
    active = tuple(hd for hd in range(N_HEADS) if all(hd not in heads for heads, _ in FAR_SKIP_CLASSES))
    pc_0 = pos_tile(0, first_final)
    for hd in active:
        scores(0, hd, 0, pc_0)
    for (heads, _), t_join in zip(FAR_SKIP_CLASSES, join_at):
        pc_join = pos_tile(jnp.minimum(t_join, qb), jnp.where(t_join == 0, first_final, 0))
        for hd in heads:
            scores(0, hd, jnp.minimum(t_join, qb), pc_join)

    def slot_loop(t_from, t_to, heads):
        def trip(i, carry):
            run_slots(t_from + unroll * i, unroll, heads)
            return carry
        lax.fori_loop(0, (t_to - t_from) // unroll, trip, 0)

    t_prev = 0
    for (heads, _), t_join in zip(FAR_SKIP_CLASSES, join_at):
        slot_loop(t_prev, t_join, active)
        active, t_prev = active + heads, t_join
    slot_loop(t_prev, n_slots, active)

    @pl.when((n_slots - t_prev) % unroll != 0)
    def _():
        run_slots(n_slots - 2, 2, active)

    def finish(buf):
        key_i = lax.broadcasted_iota(jnp.int32, (blk, blk), 0)
        qry_i = lax.broadcasted_iota(jnp.int32, (blk, blk), 1)
        vt_own = vt_ref[qb]
        for hd in range(N_HEADS):
            update(hd, jnp.where(key_i <= qry_i, s_ref[buf, hd], MASK_VALUE), vt_own)
        outs = []
        for hd in range(N_HEADS):
            acc = acc_ref[hd]
            outs.append(acc[0:HEAD_DIM, :] / acc[HEAD_DIM:HEAD_DIM + 1, :])
        o_ref[...] = jnp.concatenate(outs, axis=0).T.astype(o_ref.dtype)

    @pl.when(qb % 2 == 1)
    def _():
        phase(0, 1, vt_ref[qb - 1], qb, 1, active)
        finish(1)

    @pl.when(qb % 2 == 0)
    def _():
        finish(0)


def _moba(qt, k, vt, kmean, knorm, pos_const):
    B, S, _ = k.shape
    nb = S // MOBA_BLOCK
    return pl.pallas_call(
        _moba_kernel,
        grid=(B, nb),
        in_specs=[
            pl.BlockSpec((None, D_BRANCH, MOBA_BLOCK), lambda b, i: (b, 0, i)),
            pl.BlockSpec((None, S, D_BRANCH), lambda b, i: (b, 0, 0)),
            pl.BlockSpec((None, nb, D_BRANCH, MOBA_BLOCK), lambda b, i: (b, 0, 0, 0)),
            pl.BlockSpec((S, LANES), lambda b, i: (0, 0)),
            pl.BlockSpec((None, MAX_BLOCKS, D_BRANCH), lambda b, i: (b, 0, 0)),
            pl.BlockSpec((None, MAX_BLOCKS, LANES), lambda b, i: (b, 0, 0)),
        ],
        out_specs=pl.BlockSpec((None, MOBA_BLOCK, D_BRANCH), lambda b, i: (b, i, 0)),
        out_shape=jax.ShapeDtypeStruct((B, S, D_BRANCH), BF16),
        scratch_shapes=[
            pltpu.VMEM((N_HEADS, 2 * LANES, MOBA_BLOCK), BF16),
            pltpu.VMEM((2, N_HEADS, MOBA_BLOCK, MOBA_BLOCK), F32),
            pltpu.VMEM((N_HEADS, 1, MOBA_BLOCK), F32),
            pltpu.VMEM((N_HEADS, HEAD_DIM + _SUM_ROWS, MOBA_BLOCK), F32),
        ],
        compiler_params=pltpu.CompilerParams(
            dimension_semantics=("arbitrary", "arbitrary"), vmem_limit_bytes=VMEM_LIMIT_BYTES),
        name="moba",
    )(qt, k, vt, pos_const, kmean, knorm)


def _merge_out_kernel(x_ref, mixed_ref, gate_c_ref, att_ref, gc_ref, wbc_ref, wout_ref, g_ref, b_ref,
                      o_ref, *, alpha):
    y_c = (att_ref[...].astype(F32) * gc_ref[...].astype(F32)).astype(BF16)
    mixed = mixed_ref[...].astype(F32) + gate_c_ref[...].astype(F32) * _dot(y_c, wbc_ref[...])
    out = _dot(mixed.astype(BF16), wout_ref[...])
    o_ref[...] = _layer_norm(alpha * x_ref[...] + out, g_ref[...], b_ref[...])


def _merge_out(x, mixed_ab, gate_c, att, gc, w_branch, w_out, ln_g, ln_b, *, layer, tm, alpha):
    B, S, D = x.shape
    in_spec = lambda width: pl.BlockSpec((None, tm, width), lambda b, i: (b, i, 0), pipeline_mode=pl.Buffered(3))
    out_spec = pl.BlockSpec((None, tm, D_MODEL), lambda b, i: (b, i, 0))

    def outer(x_hbm, mixed_hbm, gate_hbm, att_hbm, gc_hbm, wbc_ref, wout_ref, g_ref, b_ref, o_hbm):
        def body(x_ref, mixed_ref, gate_ref, att_ref, gc_ref, o_ref):
            _merge_out_kernel(x_ref, mixed_ref, gate_ref, att_ref, gc_ref, wbc_ref, wout_ref, g_ref, b_ref, o_ref,
                              alpha=alpha)
        pltpu.emit_pipeline(
            body, grid=(B, S // tm),
            in_specs=[in_spec(D), in_spec(D_MODEL), in_spec(D_MODEL), in_spec(D_BRANCH), in_spec(D_BRANCH)],
            out_specs=[out_spec],
        )(x_hbm, mixed_hbm, gate_hbm, att_hbm, gc_hbm, o_hbm)

    any_spec = pl.BlockSpec(memory_space=pl.ANY)
    vmem_spec = pl.BlockSpec(memory_space=pltpu.VMEM)
    return pl.pallas_call(
        outer,
        in_specs=[any_spec] * 5 + [vmem_spec] * 4,
        out_specs=any_spec,
        out_shape=jax.ShapeDtypeStruct((B, S, D_MODEL), F32),
        compiler_params=pltpu.CompilerParams(vmem_limit_bytes=VMEM_LIMIT_BYTES),
        name="merge_out",
    )(x, mixed_ab, gate_c, att, gc, w_branch[layer, 2], w_out[layer], ln_g, ln_b)


def _position_constants(S):
    s = jnp.arange(S, dtype=jnp.int32)
    blk, pos = s // MOBA_BLOCK, s % MOBA_BLOCK
    lane = jnp.arange(LANES, dtype=jnp.int32)[None, :]
    pc = (lane == blk[:, None]).astype(F32)
    pc = jnp.where((lane >= _PC_ONE) & (lane < _PC_ONE + N_SPLIT), 1.0, pc)
    pc = jnp.where((lane >= _PC_BLK) & (lane < _PC_BLK + N_SPLIT), (blk * MOBA_BLOCK).astype(F32)[:, None], pc)
    pc = jnp.where((lane >= _PC_POS) & (lane < _PC_POS + N_SPLIT), pos.astype(F32)[:, None], pc)
    return pc.astype(BF16)


def kernel(x, w_in, sg_w, sg_b, v_ln_g, v_ln_b, conv_w, conv_b, cv_ln_g, cv_ln_b, w_branch, w_out, ln_g, ln_b):
    depth = w_in.shape[0]
    B, S, D = x.shape
    nb = S // MOBA_BLOCK
    assert D == D_MODEL and S % MOBA_BLOCK == 0 and nb <= MAX_BLOCKS == _PC_ONE
    alpha = (2 * depth) ** 0.25
    tm = 512
    tm_out = min(1024, S)
    pos_const = _position_constants(S)
    w_in_b = w_in.astype(BF16)
    w_br_b = w_branch.astype(BF16)
    w_out_b = w_out.astype(BF16)
    head_ones = (jnp.arange(D_BRANCH)[:, None] // HEAD_DIM == jnp.arange(LANES)[None, :]).astype(BF16)
    sg_b_bc = jnp.broadcast_to(sg_b[..., None], sg_b.shape + (D_BRANCH // GMLP_GROUPS,))
    row = lambda a, l: a[l][None, :]
    for l in range(depth):
        mixed_ab, gate_c, qt, k, vt, gc, kmean, knorm = _proj_mix(
            x, w_in_b, sg_w[l], sg_b_bc[l], row(v_ln_g, l), row(v_ln_b, l), conv_w[l], row(conv_b, l),
            row(cv_ln_g, l), row(cv_ln_b, l), w_br_b, head_ones, layer=l, tm=tm)
        pad_blocks = lambda a: jnp.pad(a.reshape(B, nb, -1), ((0, 0), (0, MAX_BLOCKS - nb), (0, 0)))
        att = _moba(qt, k, vt, pad_blocks(kmean), pad_blocks(knorm), pos_const)
        x = _merge_out(x, mixed_ab, gate_c, att, gc, w_br_b, w_out_b, row(ln_g, l), row(ln_b, l),
                       layer=l, tm=tm_out, alpha=alpha)
    return x
```
